```python
import jax, jax.numpy as jnp
from jax import lax
import numpy as np

D_MODEL = 2048
BATCH = 4
SEQ = 2048
DEPTH = 4

CHUNK = 64
Q_BLOCK = 128
EPS = 1e-6

D_SSM = D_MODEL // 2
SSM_HEAD_DIM = 64
SSM_HEADS = D_SSM // SSM_HEAD_DIM
SSM_GROUPS = 2
HEADS_PER_GROUP = SSM_HEADS // SSM_GROUPS
SSM_STATE = 128
CONV_WIDTH = 4
CONV_DIM = D_SSM + 2 * SSM_GROUPS * SSM_STATE
D_ATTN = D_MODEL // 2
ATTN_HEAD_DIM = 128
ATTN_HEADS = D_ATTN // ATTN_HEAD_DIM
D_MIX = D_SSM + D_ATTN

D_FF = 5632
N_MOD = 9

IN_SIZES = (D_SSM, CONV_DIM, SSM_HEADS, D_ATTN, D_ATTN, D_ATTN, ATTN_HEADS)
D_IN_PROJ = sum(IN_SIZES)
IN_SPLITS = tuple(int(s) for s in np.cumsum(IN_SIZES)[:-1])

kernel_name = 'hymba_ssd_fox_macaron_adaln'


def rmsnorm(x, g):
    xf = x.astype(jnp.float32)
    y = xf * lax.rsqrt(jnp.mean(xf * xf, axis=-1, keepdims=True) + EPS)
    return (y * g.astype(jnp.float32)).astype(x.dtype)


def group_rmsnorm(x, g, groups):
    shp = x.shape
    xf = x.astype(jnp.float32).reshape(shp[:-1] + (groups, shp[-1] // groups))
    y = xf * lax.rsqrt(jnp.mean(xf * xf, axis=-1, keepdims=True) + EPS)
    return (y.reshape(shp) * g.astype(jnp.float32)).astype(x.dtype)


def modulate(h, shift, scale):
    return h * (1 + scale[:, None, :]) + shift[:, None, :]


def swiglu(h, w_gate, w_up, w_down):
    return (jax.nn.silu(h @ w_gate) * (h @ w_up)) @ w_down


def causal_depthwise_conv(u, w, b):
    k = w.shape[0]
    out = lax.conv_general_dilated(
        u, w[:, None, :].astype(u.dtype), window_strides=(1,), padding=[(k - 1, 0)],
        dimension_numbers=('NWC', 'WIO', 'NWC'), feature_group_count=u.shape[-1])
    return out + b


def ssd_chunked(x, dt, a, b_mat, c_mat):
    bsz, seqlen, nh, hp = x.shape
    nc = seqlen // CHUNK
    xc = (x * dt[..., None]).reshape(bsz, nc, CHUNK, nh, hp)
    ac = (dt * a).reshape(bsz, nc, CHUNK, nh)
    bc = jnp.repeat(b_mat, HEADS_PER_GROUP, axis=2).reshape(bsz, nc, CHUNK, nh, -1)
    cc = jnp.repeat(c_mat, HEADS_PER_GROUP, axis=2).reshape(bsz, nc, CHUNK, nh, -1)
    a_cs = jnp.cumsum(ac, axis=2)
    a_t = jnp.swapaxes(a_cs, 2, 3)
    seg = a_t[..., :, None] - a_t[..., None, :]
    tri = jnp.tril(jnp.ones((CHUNK, CHUNK), dtype=bool))
    decay = jnp.exp(jnp.where(tri, seg, -jnp.inf))
    cb = jnp.einsum('bclhn,bcshn->bchls', cc, bc)
    y_diag = jnp.einsum('bchls,bcshp->bclhp', cb * decay, xc)
    decay_states = jnp.exp(a_cs[:, :, -1:, :] - a_cs)
    states = jnp.einsum('bclhn,bclhp->bchpn', bc, xc * decay_states[..., None])
    chunk_decay = jnp.exp(a_cs[:, :, -1, :])

    def step(h, inp):
        st, dec = inp
        return dec[:, :, None, None] * h + st, h

    init = jnp.zeros((bsz, nh, hp, states.shape[-1]), dtype=states.dtype)
    _, prev = lax.scan(step, init, (jnp.swapaxes(states, 0, 1), jnp.swapaxes(chunk_decay, 0, 1)))
    prev = jnp.swapaxes(prev, 0, 1)
    y_off = jnp.einsum('bclhn,bchpn->bclhp', cc, prev) * jnp.exp(a_cs)[..., None]
    return (y_diag + y_off).reshape(bsz, seqlen, nh, hp)


def forgetting_attention(q, k, v, log_f):
    bsz, seqlen, nh, hd = q.shape
    nb = seqlen // Q_BLOCK
    cum = jnp.cumsum(log_f, axis=1)
    cum_k = jnp.swapaxes(cum, 1, 2)
    q_blocks = jnp.swapaxes(q.reshape(bsz, nb, Q_BLOCK, nh, hd), 0, 1)
    cq_blocks = jnp.swapaxes(cum.reshape(bsz, nb, Q_BLOCK, nh), 0, 1)
    q_pos = jnp.arange(seqlen).reshape(nb, Q_BLOCK)
    k_pos = jnp.arange(seqlen)
    scale = ATTN_HEAD_DIM ** -0.5

    def block(args):
        qi, ci, pi = args
        s = jnp.einsum('bqhd,bkhd->bhqk', qi, k).astype(jnp.float32) * scale
        s = s + (jnp.swapaxes(ci, 1, 2)[..., :, None] - cum_k[..., None, :])
        s = jnp.where(k_pos[None, :] <= pi[:, None], s, -jnp.inf)
        p = jax.nn.softmax(s, axis=-1)
        return jnp.einsum('bhqk,bkhd->bqhd', p.astype(v.dtype), v)

    out = lax.map(block, (q_blocks, cq_blocks, q_pos))
    return jnp.swapaxes(out, 0, 1).reshape(bsz, seqlen, nh, hd)


def hybrid_mixer(h, w_in, conv_w, conv_b, dt_bias, a_log, d_skip, ssm_norm, f_bias, attn_norm, w_out):
    bsz, seqlen, _ = h.shape
    proj = h @ w_in
    z, xbc, dt_raw, q, k, v, f_raw = jnp.split(proj, IN_SPLITS, axis=-1)
    xbc = jax.nn.silu(causal_depthwise_conv(xbc, conv_w, conv_b))
    xs, b_mat, c_mat = jnp.split(xbc, (D_SSM, D_SSM + SSM_GROUPS * SSM_STATE), axis=-1)
    dt = jax.nn.softplus(dt_raw.astype(jnp.float32) + dt_bias.astype(jnp.float32))
    a = -jnp.exp(a_log.astype(jnp.float32))
    xs_h = xs.reshape(bsz, seqlen, SSM_HEADS, SSM_HEAD_DIM)
    y = ssd_chunked(xs_h, dt, a,
                    b_mat.reshape(bsz, seqlen, SSM_GROUPS, SSM_STATE),
                    c_mat.reshape(bsz, seqlen, SSM_GROUPS, SSM_STATE))
    y = (y + d_skip[:, None] * xs_h).reshape(bsz, seqlen, D_SSM)
    y_ssm = group_rmsnorm(y * jax.nn.silu(z), ssm_norm, SSM_GROUPS).astype(h.dtype)
    log_f = jax.nn.log_sigmoid(f_raw.astype(jnp.float32) + f_bias.astype(jnp.float32))
    o = forgetting_attention(q.reshape(bsz, seqlen, ATTN_HEADS, ATTN_HEAD_DIM),
                             k.reshape(bsz, seqlen, ATTN_HEADS, ATTN_HEAD_DIM),
                             v.reshape(bsz, seqlen, ATTN_HEADS, ATTN_HEAD_DIM), log_f)
    y_attn = group_rmsnorm(o.reshape(bsz, seqlen, D_ATTN), attn_norm, ATTN_HEADS).astype(h.dtype)
    return jnp.concatenate([y_ssm, y_attn], axis=-1) @ w_out


def setup_inputs(seed: int = 0) -> dict:
    key = jax.random.key(seed)
    ks = jax.random.split(key, 20)
    f32 = jnp.float32
    nrm = lambda k, shp, s: (jax.random.normal(k, shp, f32) * s)
    x = jax.random.normal(ks[0], (BATCH, SEQ, D_MODEL), f32)
    c = jax.random.normal(ks[1], (BATCH, D_MODEL), f32)
    norm_pre = 1.0 + nrm(ks[2], (DEPTH, 3, D_MODEL), 0.02)
    norm_post = 1.0 + nrm(ks[3], (DEPTH, 3, D_MODEL), 0.02)
    w_mod = nrm(ks[4], (DEPTH, D_MODEL, N_MOD * D_MODEL), 0.5 * D_MODEL ** -0.5)
    gate_slots = jnp.repeat(jnp.array([0.0, 0.0, 1.0] * 3, f32), D_MODEL)
    b_mod = gate_slots[None, :] + nrm(ks[5], (DEPTH, N_MOD * D_MODEL), 0.02)
    w_ffn_gate = nrm(ks[6], (DEPTH, 2, D_MODEL, D_FF), D_MODEL ** -0.5)
    w_ffn_up = nrm(ks[7], (DEPTH, 2, D_MODEL, D_FF), D_MODEL ** -0.5)
    w_ffn_down = nrm(ks[8], (DEPTH, 2, D_FF, D_MODEL), D_FF ** -0.5)
    w_in = nrm(ks[9], (DEPTH, D_MODEL, D_IN_PROJ), D_MODEL ** -0.5)
    conv_w = nrm(ks[10], (DEPTH, CONV_WIDTH, CONV_DIM), CONV_WIDTH ** -0.5)
    conv_b = nrm(ks[11], (DEPTH, CONV_DIM), 0.02)
    dt0 = jnp.exp(jax.random.uniform(ks[12], (DEPTH, SSM_HEADS), f32, np.log(1e-3), np.log(1e-1)))
    dt_bias = dt0 + jnp.log(-jnp.expm1(-dt0))
    a_log = jnp.log(jax.random.uniform(ks[13], (DEPTH, SSM_HEADS), f32, 1.0, 16.0))
    d_skip = 1.0 + nrm(ks[14], (DEPTH, SSM_HEADS), 0.1)
    ssm_norm = 1.0 + nrm(ks[15], (DEPTH, D_SSM), 0.02)
    f_bias = 2.0 + nrm(ks[16], (DEPTH, ATTN_HEADS), 0.5)
    attn_norm = 1.0 + nrm(ks[17], (DEPTH, D_ATTN), 0.02)
    w_out = nrm(ks[18], (DEPTH, D_MIX, D_MODEL), D_MIX ** -0.5)
    return {'x': x, 'c': c, 'norm_pre': norm_pre, 'norm_post': norm_post, 'w_mod': w_mod, 'b_mod': b_mod,
            'w_ffn_gate': w_ffn_gate, 'w_ffn_up': w_ffn_up, 'w_ffn_down': w_ffn_down,
            'w_in': w_in, 'conv_w': conv_w, 'conv_b': conv_b, 'dt_bias': dt_bias, 'a_log': a_log,
            'd_skip': d_skip, 'ssm_norm': ssm_norm, 'f_bias': f_bias, 'attn_norm': attn_norm, 'w_out': w_out}


def reference(x, c, norm_pre, norm_post, w_mod, b_mod, w_ffn_gate, w_ffn_up, w_ffn_down,
              w_in, conv_w, conv_b, dt_bias, a_log, d_skip, ssm_norm, f_bias, attn_norm, w_out):
    c_act = jax.nn.silu(c)
    for l in range(DEPTH):
        mod = c_act @ w_mod[l] + b_mod[l]
        sh1, sc1, g1, sh2, sc2, g2, sh3, sc3, g3 = jnp.split(mod, N_MOD, axis=-1)
        h = modulate(rmsnorm(x, norm_pre[l, 0]), sh1, sc1)
        h = rmsnorm(swiglu(h, w_ffn_gate[l, 0], w_ffn_up[l, 0], w_ffn_down[l, 0]), norm_post[l, 0])
        x = x + (0.5 * g1[:, None, :] * h).astype(x.dtype)
        h = modulate(rmsnorm(x, norm_pre[l, 1]), sh2, sc2)
        h = hybrid_mixer(h, w_in[l], conv_w[l], conv_b[l], dt_bias[l], a_log[l], d_skip[l],
                         ssm_norm[l], f_bias[l], attn_norm[l], w_out[l])
        h = rmsnorm(h, norm_post[l, 1])
        x = x + (g2[:, None, :] * h).astype(x.dtype)
        h = modulate(rmsnorm(x, norm_pre[l, 2]), sh3, sc3)
        h = rmsnorm(swiglu(h, w_ffn_gate[l, 1], w_ffn_up[l, 1], w_ffn_down[l, 1]), norm_post[l, 2])
        x = x + (0.5 * g3[:, None, :] * h).astype(x.dtype)
    return x
```

```python
import functools

import jax
import jax.numpy as jnp
from jax import lax
from jax.experimental import pallas as pl
from jax.experimental.pallas import tpu as pltpu

F32 = jnp.float32
BF16 = jnp.bfloat16

EPS = 1e-6
N_MOD = 9
SSM_HEAD_DIM = 64
SSM_HEADS = 16
SSM_GROUPS = 2
SSM_STATE = 128
CONV_WIDTH = 4
ATTN_HEAD_DIM = 128
ATTN_HEADS = 8
LANES = 128
SUBLANES = 8
VMEM_LIMIT_BYTES = 56 * 1024 * 1024

FFN_TM, FFN_TF = 512, 512
PROJ_TM, PROJ_TN = 512, 1152
OUT_TM = 512
MOD_TN = 1024
SSD_CHUNK = 128
ATTN_TQ = 256
GATE_BLK = 128


def _params(*sem):
    return pltpu.CompilerParams(dimension_semantics=sem, vmem_limit_bytes=VMEM_LIMIT_BYTES)


def _silu(x):
    return x / (1.0 + jnp.exp(-x))


def _rms(x, g):
    return x * lax.rsqrt(jnp.mean(x * x, axis=-1, keepdims=True) + EPS) * g


def _norm_mod(x, g, shift, scale):
    return _rms(x, g) * (1.0 + scale) + shift


def _split3(x):
    hi = x.astype(BF16)
    r = x - hi.astype(F32)
    mid = r.astype(BF16)
    lo = (r - mid.astype(F32)).astype(BF16)
    return hi, mid, lo


def _dot(a, b):
    return jnp.dot(a, b, preferred_element_type=F32)


def _dot3(m01, x):
    hi, mid, lo = _split3(x)
    return _dot(m01, hi) + _dot(m01, mid) + _dot(m01, lo)


def _dot3r(x, m01):
    hi, mid, lo = _split3(x)
    return _dot(hi, m01) + _dot(mid, m01) + _dot(lo, m01)


def _tril_ones(n):
    r = lax.broadcasted_iota(jnp.int32, (n, n), 0)
    c = lax.broadcasted_iota(jnp.int32, (n, n), 1)
    return r >= c


def _mod_kernel(c_ref, w_ref, b_ref, o_ref):
    ca = _silu(c_ref[...]).astype(BF16)
    o_ref[...] = _dot(ca, w_ref[...].astype(BF16)) + b_ref[...]


def _modulation(c_pad, w_mod, b_mod):
    depth, d, n = w_mod.shape
    rows = c_pad.shape[0]
    return pl.pallas_call(
        _mod_kernel,
        grid=(depth, n // MOD_TN),
        in_specs=[
            pl.BlockSpec((rows, d), lambda l, j: (0, 0)),
            pl.BlockSpec((None, d, MOD_TN), lambda l, j: (l, 0, j)),
            pl.BlockSpec((None, 1, MOD_TN), lambda l, j: (l, 0, j)),
        ],
        out_specs=pl.BlockSpec((None, rows, MOD_TN), lambda l, j: (l, 0, j)),
        out_shape=jax.ShapeDtypeStruct((depth, rows, n), F32),
        compiler_params=_params("arbitrary", "arbitrary"),
        name="modulation",
    )(c_pad, w_mod, b_mod.reshape(depth, 1, n))


def _ffn_kernel(x_ref, gpre_ref, sh_ref, sc_ref, gate_ref, gpost_ref, wg_ref, wu_ref, wd_ref,
                o_ref, h_ref):
    f = pl.program_id(1)

    @pl.when(f == 0)
    def _():
        h_ref[...] = _norm_mod(x_ref[...], gpre_ref[...], sh_ref[...], sc_ref[...]).astype(BF16)

    h = h_ref[...]
    g = _dot(h, wg_ref[...])
    u = _dot(h, wu_ref[...])
    part = _dot((_silu(g) * u).astype(BF16), wd_ref[...])

    @pl.when(f == 0)
    def _():
        o_ref[...] = part

    @pl.when(f > 0)
    def _():
        o_ref[...] += part

    @pl.when(f == pl.num_programs(1) - 1)
    def _():
        o_ref[...] = x_ref[...] + (0.5 * gate_ref[...]) * _rms(o_ref[...], gpost_ref[...])


def _vec_spec(row_fn):
    return lambda d: pl.BlockSpec((None, 1, d), lambda i, j: (row_fn(i), 0, 0))


def _ffn(x, mod, norm_pre, norm_post, wg, wu, wd, layer, slot, sub, seq, mod_rows):
    t, d = x.shape
    dff = wg.shape[-1]
    per_b = seq // FFN_TM
    mrow = lambda j: (lambda i: (layer * mod_rows + i // per_b) * N_MOD + j)
    nrow = lambda i: layer * 3 + sub
    spec = lambda fn: _vec_spec(fn)(d)
    return pl.pallas_call(
        _ffn_kernel,
        grid=(t // FFN_TM, dff // FFN_TF),
        in_specs=[
            pl.BlockSpec((FFN_TM, d), lambda i, f: (i, 0)),
            spec(nrow), spec(mrow(3 * sub)), spec(mrow(3 * sub + 1)), spec(mrow(3 * sub + 2)),
            spec(nrow),
            pl.BlockSpec((None, None, d, FFN_TF), lambda i, f: (layer, slot, 0, f)),
            pl.BlockSpec((None, None, d, FFN_TF), lambda i, f: (layer, slot, 0, f)),
            pl.BlockSpec((None, None, FFN_TF, d), lambda i, f: (layer, slot, f, 0)),
        ],
        out_specs=pl.BlockSpec((FFN_TM, d), lambda i, f: (i, 0)),
        out_shape=jax.ShapeDtypeStruct((t, d), F32),
        scratch_shapes=[pltpu.VMEM((FFN_TM, d), BF16)],
        compiler_params=_params("arbitrary", "arbitrary"),
        name="swiglu_half_step",
    )(x, norm_pre, mod, mod, mod, norm_post, wg, wu, wd)


def _proj_kernel(x_ref, gpre_ref, sh_ref, sc_ref, w_ref, o_ref, h_ref):
    @pl.when(pl.program_id(1) == 0)
    def _():
        h_ref[...] = _norm_mod(x_ref[...], gpre_ref[...], sh_ref[...], sc_ref[...]).astype(BF16)

    o_ref[...] = _dot(h_ref[...], w_ref[...])


def _in_proj(x, mod, norm_pre, w_in_r, layer, seq, mod_rows):
    t, d = x.shape
    n = w_in_r.shape[-1]
    per_b = seq // PROJ_TM
    mrow = lambda j: (lambda i: (layer * mod_rows + i // per_b) * N_MOD + j)
    spec = lambda fn: _vec_spec(fn)(d)
    return pl.pallas_call(
        _proj_kernel,
        grid=(t // PROJ_TM, n // PROJ_TN),
        in_specs=[
            pl.BlockSpec((PROJ_TM, d), lambda i, j: (i, 0)),
            spec(lambda i: layer * 3 + 1), spec(mrow(3)), spec(mrow(4)),
            pl.BlockSpec((None, d, PROJ_TN), lambda i, j: (layer, 0, j)),
        ],
        out_specs=pl.BlockSpec((PROJ_TM, PROJ_TN), lambda i, j: (i, j)),
        out_shape=jax.ShapeDtypeStruct((t, n), F32),
        scratch_shapes=[pltpu.VMEM((PROJ_TM, d), BF16)],
        compiler_params=_params("arbitrary", "arbitrary"),
        name="mixer_in_proj",
    )(x, norm_pre, mod, mod, w_in_r)


def _gates_kernel(p_ref, bias_ref, dt_ref, cum_ref, cumt_ref):
    seq = p_ref.shape[0]
    tril = _tril_ones(GATE_BLK).astype(BF16)
    carry = jnp.zeros((1, LANES), F32)
    for blk in range(seq // GATE_BLK):
        rows = pl.ds(blk * GATE_BLK, GATE_BLK)
        u = p_ref[rows, :] + bias_ref[...]
        t = jnp.log1p(jnp.exp(-jnp.abs(u)))
        dt_ref[rows, :] = jnp.maximum(u, 0.0) + t
        log_f = jnp.minimum(u, 0.0) - t
        cs = _dot3(tril, log_f) + carry
        cum_ref[rows, :] = cs
        cumt_ref[:, rows] = cs.T[2 * SUBLANES:2 * SUBLANES + ATTN_HEADS, :]
        carry = cs[GATE_BLK - 1:GATE_BLK, :]


def _gates(proj, gate_bias, layer, batch, seq, col_block):
    t = proj.shape[0]
    return pl.pallas_call(
        _gates_kernel,
        grid=(batch,),
        in_specs=[
            pl.BlockSpec((seq, LANES), lambda b: (b, col_block)),
            pl.BlockSpec((None, 1, LANES), lambda b: (layer, 0, 0)),
        ],
        out_specs=[
            pl.BlockSpec((seq, LANES), lambda b: (b, 0)),
            pl.BlockSpec((seq, LANES), lambda b: (b, 0)),
            pl.BlockSpec((None, ATTN_HEADS, seq), lambda b: (b, 0, 0)),
        ],
        out_shape=[
            jax.ShapeDtypeStruct((t, LANES), F32),
            jax.ShapeDtypeStruct((t, LANES), F32),
            jax.ShapeDtypeStruct((batch, ATTN_HEADS, seq), F32),
        ],
        compiler_params=_params("arbitrary"),
        name="mixer_gates",
    )(proj, gate_bias)


def _ssd_kernel(xs_ref, bc_ref, z_ref, dt_ref, cw_ref, cb_ref, alog_ref, dskip_ref, gn_ref,
                o_ref, ubuf_ref, state_ref, ydiag_ref):
    lc = xs_ref.shape[0]
    d_ssm = xs_ref.shape[1]
    gw = d_ssm // SSM_GROUPS
    hpg = SSM_HEADS // SSM_GROUPS

    @pl.when(pl.program_id(1) == 0)
    def _():
        ubuf_ref[0:SUBLANES, :] = jnp.zeros((SUBLANES, ubuf_ref.shape[1]), F32)
        state_ref[...] = jnp.zeros(state_ref.shape, F32)

    ubuf_ref[SUBLANES:, 0:d_ssm] = xs_ref[...]
    ubuf_ref[SUBLANES:, d_ssm:] = bc_ref[...]
    conv = cb_ref[...] + sum(
        cw_ref[k:k + 1, :] * ubuf_ref[pl.ds(SUBLANES - CONV_WIDTH + 1 + k, lc), :]
        for k in range(CONV_WIDTH))
    ubuf_ref[0:SUBLANES, :] = ubuf_ref[lc:lc + SUBLANES, :]
    xbc = _silu(conv)
    xs = xbc[:, 0:d_ssm]
    b_mat = xbc[:, d_ssm:d_ssm + SSM_GROUPS * SSM_STATE].astype(BF16)
    c_mat = xbc[:, d_ssm + SSM_GROUPS * SSM_STATE:].astype(BF16)

    lane = lax.broadcasted_iota(jnp.int32, (1, LANES), 1)
    a = jnp.where(lane < SSM_HEADS, -jnp.exp(alog_ref[...]), 0.0)
    dt = dt_ref[...]
    tri = _tril_ones(lc)
    a_cs = _dot3(tri.astype(BF16), dt * a)
    a_cs_t = a_cs.T
    a_last = a_cs[lc - 1:lc, :]

    er = lax.broadcasted_iota(jnp.int32, (LANES, d_ssm), 0)
    ec = lax.broadcasted_iota(jnp.int32, (LANES, d_ssm), 1)
    expand = (ec // SSM_HEAD_DIM == er).astype(BF16)
    dt_x = _dot3r(dt, expand)
    dec_in = _dot3r(jnp.exp(a_cs), expand)
    dec_out = _dot3r(jnp.exp(a_last - a_cs), expand)
    xdt = xs * dt_x

    for g in range(SSM_GROUPS):
        cg = c_mat[:, g * SSM_STATE:(g + 1) * SSM_STATE]
        bg = b_mat[:, g * SSM_STATE:(g + 1) * SSM_STATE]
        cb = lax.dot_general(cg, bg, (((1,), (1,)), ((), ())), preferred_element_type=F32)
        for hh in range(hpg):
            h = g * hpg + hh
            seg = a_cs[:, h:h + 1] - a_cs_t[h:h + 1, :]
            m = (cb * jnp.exp(jnp.where(tri, seg, -jnp.inf))).astype(BF16)
            cols = slice(h * SSM_HEAD_DIM, (h + 1) * SSM_HEAD_DIM)
            ydiag_ref[:, cols] = _dot(m, xdt[:, cols].astype(BF16))

    xw = (xdt * dec_out).astype(BF16)
    y_parts = []
    for g in range(SSM_GROUPS):
        cols = slice(g * gw, (g + 1) * gw)
        cg = c_mat[:, g * SSM_STATE:(g + 1) * SSM_STATE]
        bg = b_mat[:, g * SSM_STATE:(g + 1) * SSM_STATE]
        st = state_ref[g]
        y_parts.append(_dot(cg, st.astype(BF16)))
        upd = lax.dot_general(bg, xw[:, cols], (((0,), (0,)), ((), ())), preferred_element_type=F32)
        state_ref[g] = st * dec_in[lc - 1:lc, cols] + upd
    y_off = jnp.concatenate(y_parts, axis=1) * dec_in

    y = (ydiag_ref[...] + y_off + dskip_ref[...] * xs) * _silu(z_ref[...])
    outs = []
    for g in range(SSM_GROUPS):
        cols = slice(g * gw, (g + 1) * gw)
        outs.append(_rms(y[:, cols], gn_ref[:, cols]))
    o_ref[...] = jnp.concatenate(outs, axis=1).astype(o_ref.dtype)


def _ssd(proj, dt_sp, conv_w, conv_b, a_log_pad, d_skip_x, ssm_norm, layer, batch, seq):
    t = proj.shape[0]
    d_ssm = SSM_HEADS * SSM_HEAD_DIM
    d_bc = 2 * SSM_GROUPS * SSM_STATE
    conv_dim = d_ssm + d_bc
    nc = seq // SSD_CHUNK
    row = lambda b, c: b * nc + c
    par = lambda width: pl.BlockSpec((None, 1, width), lambda b, c: (layer, 0, 0))
    return pl.pallas_call(
        _ssd_kernel,
        grid=(batch, nc),
        in_specs=[
            pl.BlockSpec((SSD_CHUNK, d_ssm), lambda b, c: (row(b, c), 1)),
            pl.BlockSpec((SSD_CHUNK, d_bc), lambda b, c: (row(b, c), 2 * d_ssm // d_bc)),
            pl.BlockSpec((SSD_CHUNK, d_ssm), lambda b, c: (row(b, c), 0)),
            pl.BlockSpec((SSD_CHUNK, LANES), lambda b, c: (row(b, c), 0)),
            pl.BlockSpec((None, CONV_WIDTH, conv_dim), lambda b, c: (layer, 0, 0)),
            par(conv_dim), par(LANES), par(d_ssm), par(d_ssm),
        ],
        out_specs=pl.BlockSpec((SSD_CHUNK, d_ssm), lambda b, c: (row(b, c), 0)),
        out_shape=jax.ShapeDtypeStruct((t, d_ssm), BF16),
        scratch_shapes=[
            pltpu.VMEM((SSD_CHUNK + SUBLANES, conv_dim), F32),
            pltpu.VMEM((SSM_GROUPS, SSM_STATE, d_ssm // SSM_GROUPS), F32),
            pltpu.VMEM((SSD_CHUNK, d_ssm), F32),
        ],
        compiler_params=_params("arbitrary", "arbitrary"),
        name="ssd_heads",
    )(proj, proj, proj, dt_sp, conv_w, conv_b, a_log_pad, d_skip_x, ssm_norm)


def _attn_kernel(q_ref, k_ref, v_ref, cum_ref, cumt_ref, gn_ref, o_ref, kb_ref, vb_ref):
    h = pl.program_id(1)
    i = pl.program_id(2)
    tq = q_ref.shape[0]

    @pl.when(i == 0)
    def _():
        kb_ref[...] = k_ref[...].astype(BF16)
        vb_ref[...] = v_ref[...].astype(BF16)

    q = (q_ref[...] * (ATTN_HEAD_DIM ** -0.5)).astype(BF16)
    lane = lax.broadcasted_iota(jnp.int32, (tq, LANES), 1)
    cq = jnp.sum(jnp.where(lane == 2 * SUBLANES + h, cum_ref[...], 0.0), axis=1, keepdims=True)

    def scores(j):
        cols = pl.ds(pl.multiple_of(j * tq, tq), tq)
        s = lax.dot_general(q, kb_ref[cols, :], (((1,), (1,)), ((), ())), preferred_element_type=F32)
        return s + (cq - cumt_ref[pl.ds(h, 1), cols])

    def update(s, j, carry):
        m, l, acc = carry
        m_new = jnp.maximum(m, jnp.max(s, axis=1, keepdims=True))
        alpha = jnp.exp(m - m_new)
        p = jnp.exp(s - m_new)
        cols = pl.ds(pl.multiple_of(j * tq, tq), tq)
        acc = alpha * acc + _dot(p.astype(BF16), vb_ref[cols, :])
        return m_new, alpha * l + jnp.sum(p, axis=1, keepdims=True), acc

    carry = (jnp.full((tq, 1), -jnp.inf, F32), jnp.zeros((tq, 1), F32),
             jnp.zeros((tq, ATTN_HEAD_DIM), F32))
    carry = lax.fori_loop(0, i, lambda j, c: update(scores(j), j, c), carry)
    causal = _tril_ones(tq)
    _, l, acc = update(jnp.where(causal, scores(i), -jnp.inf), i, carry)
    o_ref[...] = _rms(acc / l, gn_ref[...]).astype(o_ref.dtype)


def _attention(proj, cum, cum_t, attn_norm, layer, batch, seq, q_blk, k_blk, v_blk):
    t = proj.shape[0]
    nq = seq // ATTN_TQ
    d_attn = ATTN_HEADS * ATTN_HEAD_DIM
    return pl.pallas_call(
        _attn_kernel,
        grid=(batch, ATTN_HEADS, nq),
        in_specs=[
            pl.BlockSpec((ATTN_TQ, ATTN_HEAD_DIM), lambda b, h, i: (b * nq + i, q_blk + h)),
            pl.BlockSpec((seq, ATTN_HEAD_DIM), lambda b, h, i: (b, k_blk + h)),
            pl.BlockSpec((seq, ATTN_HEAD_DIM), lambda b, h, i: (b, v_blk + h)),
            pl.BlockSpec((ATTN_TQ, LANES), lambda b, h, i: (b * nq + i, 0)),
            pl.BlockSpec((None, ATTN_HEADS, seq), lambda b, h, i: (b, 0, 0)),
            pl.BlockSpec((None, 1, ATTN_HEAD_DIM), lambda b, h, i: (layer, 0, h)),
        ],
        out_specs=pl.BlockSpec((ATTN_TQ, ATTN_HEAD_DIM), lambda b, h, i: (b * nq + i, h)),
        out_shape=jax.ShapeDtypeStruct((t, d_attn), BF16),
        scratch_shapes=[pltpu.VMEM((seq, ATTN_HEAD_DIM), BF16), pltpu.VMEM((seq, ATTN_HEAD_DIM), BF16)],
        compiler_params=_params("arbitrary", "arbitrary", "arbitrary"),
        name="forgetting_attention",
    )(proj, proj, proj, cum, cum_t, attn_norm)


def _out_kernel(x_ref, ys_ref, ya_ref, w_ref, gate_ref, gpost_ref, o_ref):
    d_ssm = ys_ref.shape[1]
    h = _dot(ys_ref[...], w_ref[0:d_ssm, :]) + _dot(ya_ref[...], w_ref[d_ssm:, :])
    o_ref[...] = x_ref[...] + gate_ref[...] * _rms(h, gpost_ref[...])


def _out_proj(x, y_ssm, y_attn, w_out, mod, norm_post, layer, seq, mod_rows):
    t, d = x.shape
    per_b = seq // OUT_TM
    spec1 = lambda fn: pl.BlockSpec((None, 1, d), lambda i: (fn(i), 0, 0))
    return pl.pallas_call(
        _out_kernel,
        grid=(t // OUT_TM,),
        in_specs=[
            pl.BlockSpec((OUT_TM, d), lambda i: (i, 0)),
            pl.BlockSpec((OUT_TM, y_ssm.shape[1]), lambda i: (i, 0)),
            pl.BlockSpec((OUT_TM, y_attn.shape[1]), lambda i: (i, 0)),
            pl.BlockSpec((None,) + w_out.shape[1:], lambda i: (layer, 0, 0)),
            spec1(lambda i: (layer * mod_rows + i // per_b) * N_MOD + 5),
            spec1(lambda i: layer * 3 + 1),
        ],
        out_specs=pl.BlockSpec((OUT_TM, d), lambda i: (i, 0)),
        out_shape=jax.ShapeDtypeStruct((t, d), F32),
        compiler_params=_params("arbitrary"),
        name="mixer_out_proj",
    )(x, y_ssm, y_attn, w_out, mod, norm_post)


def kernel(x, c, norm_pre, norm_post, w_mod, b_mod, w_ffn_gate, w_ffn_up, w_ffn_down, w_in, conv_w,
           conv_b, dt_bias, a_log, d_skip, ssm_norm, f_bias, attn_norm, w_out):
    batch, seq, d = x.shape
    depth = w_mod.shape[0]
    d_ssm = SSM_HEADS * SSM_HEAD_DIM
    d_attn = ATTN_HEADS * ATTN_HEAD_DIM
    conv_dim = d_ssm + 2 * SSM_GROUPS * SSM_STATE
    assert seq % max(FFN_TM, PROJ_TM, OUT_TM, ATTN_TQ, SSD_CHUNK) == 0
    assert w_in.shape[-1] == 2 * d_ssm + 2 * SSM_GROUPS * SSM_STATE + SSM_HEADS + 3 * d_attn + ATTN_HEADS

    wg, wu, wd = (w.astype(BF16) for w in (w_ffn_gate, w_ffn_up, w_ffn_down))
    wo = w_out.astype(BF16)
    o_dt = d_ssm + conv_dim
    o_q = o_dt + SSM_HEADS
    o_f = o_q + 3 * d_attn
    pad = LANES - SSM_HEADS - ATTN_HEADS
    w_in_r = jnp.concatenate(
        [w_in[..., :o_dt], w_in[..., o_q:o_f], w_in[..., o_dt:o_q], w_in[..., o_f:],
         jnp.zeros(w_in.shape[:-1] + (pad,), w_in.dtype)], axis=-1).astype(BF16)
    assert w_in_r.shape[-1] % PROJ_TN == 0
    q_blk = o_dt // LANES
    k_blk = q_blk + d_attn // LANES
    v_blk = k_blk + d_attn // LANES
    gate_blk = v_blk + d_attn // LANES

    gate_bias = jnp.concatenate([dt_bias, f_bias, jnp.zeros((depth, pad), F32)], axis=-1)[:, None, :]
    a_log_pad = jnp.pad(a_log, ((0, 0), (0, LANES - SSM_HEADS)))[:, None, :]
    d_skip_x = jnp.repeat(d_skip, SSM_HEAD_DIM, axis=-1)[:, None, :]
    conv_b3 = conv_b[:, None, :]
    ssm_norm3 = ssm_norm[:, None, :]
    attn_norm3 = attn_norm[:, None, :]
    norm_pre_t = norm_pre.reshape(depth * 3, 1, d)
    norm_post_t = norm_post.reshape(depth * 3, 1, d)

    mod_rows = 2 * SUBLANES
    c_pad = jnp.pad(c, ((0, mod_rows - batch), (0, 0)))
    mod = _modulation(c_pad, w_mod, b_mod).reshape(depth * mod_rows * N_MOD, 1, d)

    xt = x.reshape(batch * seq, d)
    for l in range(depth):
        xt = _ffn(xt, mod, norm_pre_t, norm_post_t, wg, wu, wd, l, 0, 0, seq, mod_rows)
        proj = _in_proj(xt, mod, norm_pre_t, w_in_r, l, seq, mod_rows)
        dt_sp, cum, cum_t = _gates(proj, gate_bias, l, batch, seq, gate_blk)
        y_ssm = _ssd(proj, dt_sp, conv_w, conv_b3, a_log_pad, d_skip_x, ssm_norm3, l, batch, seq)
        y_attn = _attention(proj, cum, cum_t, attn_norm3, l, batch, seq, q_blk, k_blk, v_blk)
        xt = _out_proj(xt, y_ssm, y_attn, wo, mod, norm_post_t, l, seq, mod_rows)
        xt = _ffn(xt, mod, norm_pre_t, norm_post_t, wg, wu, wd, l, 1, 2, seq, mod_rows)
    return xt.reshape(batch, seq, d)
```

```python
import functools

import jax
import jax.numpy as jnp
from jax import lax
from jax.experimental import pallas as pl
from jax.experimental.pallas import tpu as pltpu

F32 = jnp.float32
BF16 = jnp.bfloat16

EPS = 1e-6
N_MOD = 9
SSM_HEAD_DIM = 64
SSM_HEADS = 16
SSM_GROUPS = 2
SSM_STATE = 128
CONV_WIDTH = 4
ATTN_HEAD_DIM = 128
ATTN_HEADS = 8
LANES = 128
SUBLANES = 8
VMEM_LIMIT_BYTES = 56 * 1024 * 1024

FFN_TM, FFN_TF, FFN_TN = 512, 512, 512
ROW_CHUNK, ROW_UNROLL = 16, 4
PROJ_TM = 256
OUT_TM = 512
MOD_TN = 1024
SSD_CHUNK = 128
ATTN_TQ = 512
LOG2E = 1.4426950408889634
GATE_BLK = 128


def _params(*sem):
    return pltpu.CompilerParams(dimension_semantics=sem, vmem_limit_bytes=VMEM_LIMIT_BYTES)


def _silu(x):
    return x / (1.0 + jnp.exp(-x))


def _rms(x, g):
    return x * lax.rsqrt(jnp.mean(x * x, axis=-1, keepdims=True) + EPS) * g


def _norm_mod(x, g, shift, scale):
    return _rms(x, g) * (1.0 + scale) + shift


def _split3(x):
    hi = x.astype(BF16)
    r = x - hi.astype(F32)
    mid = r.astype(BF16)
    lo = (r - mid.astype(F32)).astype(BF16)
    return hi, mid, lo


def _dot(a, b):
    return jnp.dot(a, b, preferred_element_type=F32)


def _dot3(m01, x):
    hi, mid, lo = _split3(x)
    return _dot(m01, hi) + _dot(m01, mid) + _dot(m01, lo)


def _dot3r(x, m01):
    hi, mid, lo = _split3(x)
    return _dot(hi, m01) + _dot(mid, m01) + _dot(lo, m01)


def _tril_ones(n):
    r = lax.broadcasted_iota(jnp.int32, (n, n), 0)
    c = lax.broadcasted_iota(jnp.int32, (n, n), 1)
    return r >= c


def _mod_kernel(c_ref, w_ref, b_ref, o_ref):
    ca = _silu(c_ref[...]).astype(BF16)
    o_ref[...] = _dot(ca, w_ref[...].astype(BF16)) + b_ref[...]


def _modulation(c_pad, w_mod, b_mod):
    depth, d, n = w_mod.shape
    rows = c_pad.shape[0]
    return pl.pallas_call(
        _mod_kernel,
        grid=(depth, n // MOD_TN),
        in_specs=[
            pl.BlockSpec((rows, d), lambda l, j: (0, 0)),
            pl.BlockSpec((None, d, MOD_TN), lambda l, j: (l, 0, j)),
            pl.BlockSpec((None, 1, MOD_TN), lambda l, j: (l, 0, j)),
        ],
        out_specs=pl.BlockSpec((None, rows, MOD_TN), lambda l, j: (l, 0, j)),
        out_shape=jax.ShapeDtypeStruct((depth, rows, n), F32),
        compiler_params=_params("arbitrary", "arbitrary"),
        name="modulation",
    )(c_pad, w_mod, b_mod.reshape(depth, 1, n))


def _norm_mod_rows(x_ref, g_ref, sh_ref, sc_ref, gs_ref, h_ref):
    gs_ref[...] = g_ref[...] * (1.0 + sc_ref[...])

    def body(r, carry):
        rows = pl.ds(pl.multiple_of(r * ROW_CHUNK, ROW_CHUNK), ROW_CHUNK)
        xv = x_ref[rows, :]
        rs = lax.rsqrt(jnp.mean(xv * xv, axis=-1, keepdims=True) + EPS)
        h_ref[rows, :] = (xv * rs * gs_ref[...] + sh_ref[...]).astype(BF16)
        return carry

    lax.fori_loop(0, x_ref.shape[0] // ROW_CHUNK, body, 0, unroll=ROW_UNROLL)


def _ffn_kernel(x_ref, gpre_ref, sh_ref, sc_ref, gate_ref, gpost_ref, wg_ref, wu_ref, wd_ref,
                o_ref, h_ref, a_ref, y_ref, gs_ref):
    s = pl.program_id(1)
    nf, tm, _ = a_ref.shape
    nd, _, tn = y_ref.shape
    d = x_ref.shape[1]

    @pl.when(s == 0)
    def _():
        _norm_mod_rows(x_ref, gpre_ref, sh_ref, sc_ref, gs_ref, h_ref)

    @pl.when(s < nf)
    def _():
        h = h_ref[...]
        g = _dot(h, wg_ref[...])
        u = _dot(h, wu_ref[...])
        a_ref[s] = (_silu(g) * u).astype(BF16)

    @pl.when(s >= nf)
    def _():
        a = jnp.concatenate([a_ref[k] for k in range(nf)], axis=1)
        y_ref[s - nf] = _dot(a, wd_ref[...])

    @pl.when(s == nf + nd - 1)
    def _():
        gs_ref[...] = (0.5 * gate_ref[...]) * gpost_ref[...]

        def body(r, carry):
            rows = pl.ds(pl.multiple_of(r * ROW_CHUNK, ROW_CHUNK), ROW_CHUNK)
            ys = [y_ref[n, rows, :] for n in range(nd)]
            ss = sum(jnp.sum(y * y, axis=-1, keepdims=True) for y in ys)
            rs = lax.rsqrt(ss * (1.0 / d) + EPS)
            for n in range(nd):
                cols = slice(n * tn, (n + 1) * tn)
                o_ref[rows, cols] = x_ref[rows, cols] + ys[n] * rs * gs_ref[:, cols]
            return carry

        lax.fori_loop(0, tm // ROW_CHUNK, body, 0, unroll=ROW_UNROLL)


def _vec_spec(row_fn):
    return lambda d: pl.BlockSpec((None, 1, d), lambda i, j: (row_fn(i), 0, 0))


def _ffn(x, mod, norm_pre, norm_post, wg, wu, wd, layer, slot, sub, seq, mod_rows):
    t, d = x.shape
    dff = wg.shape[-1]
    nf, nd = dff // FFN_TF, d // FFN_TN
    per_b = seq // FFN_TM
    mrow = lambda j: (lambda i: (layer * mod_rows + i // per_b) * N_MOD + j)
    nrow = lambda i: layer * 3 + sub
    spec = lambda fn: _vec_spec(fn)(d)
    up_tile = lambda i, s: (layer, slot, 0, jnp.minimum(s, nf - 1))
    return pl.pallas_call(
        _ffn_kernel,
        grid=(t // FFN_TM, nf + nd),
        in_specs=[
            pl.BlockSpec((FFN_TM, d), lambda i, s: (i, 0)),
            spec(nrow), spec(mrow(3 * sub)), spec(mrow(3 * sub + 1)), spec(mrow(3 * sub + 2)),
            spec(nrow),
            pl.BlockSpec((None, None, d, FFN_TF), up_tile),
            pl.BlockSpec((None, None, d, FFN_TF), up_tile),
            pl.BlockSpec((None, None, dff, FFN_TN), lambda i, s: (layer, slot, 0, jnp.maximum(s - nf, 0))),
        ],
        out_specs=pl.BlockSpec((FFN_TM, d), lambda i, s: (i, 0)),
        out_shape=jax.ShapeDtypeStruct((t, d), F32),
        scratch_shapes=[
            pltpu.VMEM((FFN_TM, d), BF16),
            pltpu.VMEM((nf, FFN_TM, FFN_TF), BF16),
            pltpu.VMEM((nd, FFN_TM, FFN_TN), F32),
            pltpu.VMEM((1, d), F32),
        ],
        compiler_params=_params("arbitrary", "arbitrary"),
        name="swiglu_half_step",
    )(x, norm_pre, mod, mod, mod, norm_post, wg, wu, wd)


def _in_proj(x, mod, norm_pre, weights, layer, seq, mod_rows):
    t, d = x.shape
    per_b = seq // PROJ_TM
    row1 = lambda fn: pl.BlockSpec((None, 1, d), lambda i: (fn(i), 0, 0))
    mrow = lambda j: (lambda i: (layer * mod_rows + i // per_b) * N_MOD + j)
    nw = len(weights)

    def body(x_ref, gpre_ref, sh_ref, sc_ref, *rest):
        w_refs, o_refs, (h_ref, gs_ref) = rest[:nw], rest[nw:2 * nw], rest[2 * nw:]
        _norm_mod_rows(x_ref, gpre_ref, sh_ref, sc_ref, gs_ref, h_ref)
        h = h_ref[...]
        for w_ref, o_ref in zip(w_refs, o_refs):
            o_ref[...] = _dot(h, w_ref[...])

    return pl.pallas_call(
        body,
        grid=(t // PROJ_TM,),
        in_specs=[
            pl.BlockSpec((PROJ_TM, d), lambda i: (i, 0)),
            row1(lambda i: layer * 3 + 1), row1(mrow(3)), row1(mrow(4)),
        ] + [pl.BlockSpec((None,) + w.shape[1:], lambda i: (layer, 0, 0), pipeline_mode=pl.Buffered(1))
             for w in weights],
        out_specs=[pl.BlockSpec((PROJ_TM, w.shape[-1]), lambda i: (i, 0)) for w in weights],
        out_shape=[jax.ShapeDtypeStruct((t, w.shape[-1]), F32) for w in weights],
        scratch_shapes=[pltpu.VMEM((PROJ_TM, d), BF16), pltpu.VMEM((1, d), F32)],
        compiler_params=_params("arbitrary"),
        name="mixer_in_proj",
    )(x, norm_pre, mod, mod, *weights)


def _gates_kernel(p_ref, bias_ref, dt_ref, cum_ref):
    seq = p_ref.shape[0]
    tril = _tril_ones(GATE_BLK).astype(BF16)
    carry = jnp.zeros((1, LANES), F32)
    for blk in range(seq // GATE_BLK):
        rows = pl.ds(blk * GATE_BLK, GATE_BLK)
        u = p_ref[rows, :] + bias_ref[...]
        t = jnp.log1p(jnp.exp(-jnp.abs(u)))
        dt_ref[rows, :] = jnp.maximum(u, 0.0) + t
        log_f = jnp.minimum(u, 0.0) - t
        cs = _dot3(tril, log_f) + carry
        cum_ref[rows, :] = cs
        carry = cs[GATE_BLK - 1:GATE_BLK, :]


def _gates(proj, gate_bias, layer, batch, seq):
    t = proj.shape[0]
    return pl.pallas_call(
        _gates_kernel,
        grid=(batch,),
        in_specs=[
            pl.BlockSpec((seq, LANES), lambda b: (b, 0)),
            pl.BlockSpec((None, 1, LANES), lambda b: (layer, 0, 0)),
        ],
        out_specs=[
            pl.BlockSpec((seq, LANES), lambda b: (b, 0)),
            pl.BlockSpec((seq, LANES), lambda b: (b, 0)),
        ],
        out_shape=[
            jax.ShapeDtypeStruct((t, LANES), F32),
            jax.ShapeDtypeStruct((t, LANES), F32),
        ],
        compiler_params=_params("arbitrary"),
        name="mixer_gates",
    )(proj, gate_bias)


def _ssd_kernel(xs_ref, bc_ref, z_ref, dt_ref, cw_ref, cb_ref, alog_ref, dskip_ref, gn_ref,
                o_ref, ubuf_ref, state_ref, ydiag_ref):
    lc = xs_ref.shape[0]
    d_ssm = xs_ref.shape[1]
    gw = d_ssm // SSM_GROUPS
    hpg = SSM_HEADS // SSM_GROUPS

    @pl.when(pl.program_id(1) == 0)
    def _():
        ubuf_ref[0:SUBLANES, :] = jnp.zeros((SUBLANES, ubuf_ref.shape[1]), F32)
        state_ref[...] = jnp.zeros(state_ref.shape, F32)

    ubuf_ref[SUBLANES:, 0:d_ssm] = xs_ref[...]
    ubuf_ref[SUBLANES:, d_ssm:] = bc_ref[...]
    conv = cb_ref[...] + sum(
        cw_ref[k:k + 1, :] * ubuf_ref[pl.ds(SUBLANES - CONV_WIDTH + 1 + k, lc), :]
        for k in range(CONV_WIDTH))
    ubuf_ref[0:SUBLANES, :] = ubuf_ref[lc:lc + SUBLANES, :]
    xbc = _silu(conv)
    xs = xbc[:, 0:d_ssm]
    b_mat = xbc[:, d_ssm:d_ssm + SSM_GROUPS * SSM_STATE].astype(BF16)
    c_mat = xbc[:, d_ssm + SSM_GROUPS * SSM_STATE:].astype(BF16)

    lane = lax.broadcasted_iota(jnp.int32, (1, LANES), 1)
    a = jnp.where(lane < SSM_HEADS, -jnp.exp(alog_ref[...]), 0.0)
    dt = dt_ref[...]
    tri = _tril_ones(lc)
    a_cs = _dot3(tri.astype(BF16), dt * a)
    a_cs_t = a_cs.T
    a_last = a_cs[lc - 1:lc, :]

    er = lax.broadcasted_iota(jnp.int32, (LANES, d_ssm), 0)
    ec = lax.broadcasted_iota(jnp.int32, (LANES, d_ssm), 1)
    expand = (ec // SSM_HEAD_DIM == er).astype(BF16)
    dt_x = _dot3r(dt, expand)
    dec_in = _dot3r(jnp.exp(a_cs), expand)
    dec_out = _dot3r(jnp.exp(a_last - a_cs), expand)
    xdt = xs * dt_x

    for g in range(SSM_GROUPS):
        cg = c_mat[:, g * SSM_STATE:(g + 1) * SSM_STATE]
        bg = b_mat[:, g * SSM_STATE:(g + 1) * SSM_STATE]
        cb = lax.dot_general(cg, bg, (((1,), (1,)), ((), ())), preferred_element_type=F32)
        for hh in range(hpg):
            h = g * hpg + hh
            seg = a_cs[:, h:h + 1] - a_cs_t[h:h + 1, :]
            m = (cb * jnp.exp(jnp.where(tri, seg, -jnp.inf))).astype(BF16)
            cols = slice(h * SSM_HEAD_DIM, (h + 1) * SSM_HEAD_DIM)
            ydiag_ref[:, cols] = _dot(m, xdt[:, cols].astype(BF16))

    xw = (xdt * dec_out).astype(BF16)
    y_parts = []
    for g in range(SSM_GROUPS):
        cols = slice(g * gw, (g + 1) * gw)
        cg = c_mat[:, g * SSM_STATE:(g + 1) * SSM_STATE]
        bg = b_mat[:, g * SSM_STATE:(g + 1) * SSM_STATE]
        st = state_ref[g]
        y_parts.append(_dot(cg, st.astype(BF16)))
        upd = lax.dot_general(bg, xw[:, cols], (((0,), (0,)), ((), ())), preferred_element_type=F32)
        state_ref[g] = st * dec_in[lc - 1:lc, cols] + upd
    y_off = jnp.concatenate(y_parts, axis=1) * dec_in

    y = (ydiag_ref[...] + y_off + dskip_ref[...] * xs) * _silu(z_ref[...])
    outs = []
    for g in range(SSM_GROUPS):
        cols = slice(g * gw, (g + 1) * gw)
        outs.append(_rms(y[:, cols], gn_ref[:, cols]))
    o_ref[...] = jnp.concatenate(outs, axis=1).astype(o_ref.dtype)


def _ssd(proj, dt_sp, conv_w, conv_b, a_log_pad, d_skip_x, ssm_norm, layer, batch, seq):
    t = proj.shape[0]
    d_ssm = SSM_HEADS * SSM_HEAD_DIM
    d_bc = 2 * SSM_GROUPS * SSM_STATE
    conv_dim = d_ssm + d_bc
    nc = seq // SSD_CHUNK
    row = lambda b, c: b * nc + c
    par = lambda width: pl.BlockSpec((None, 1, width), lambda b, c: (layer, 0, 0))
    return pl.pallas_call(
        _ssd_kernel,
        grid=(batch, nc),
        in_specs=[
            pl.BlockSpec((SSD_CHUNK, d_ssm), lambda b, c: (row(b, c), 1)),
            pl.BlockSpec((SSD_CHUNK, d_bc), lambda b, c: (row(b, c), 2 * d_ssm // d_bc)),
            pl.BlockSpec((SSD_CHUNK, d_ssm), lambda b, c: (row(b, c), 0)),
            pl.BlockSpec((SSD_CHUNK, LANES), lambda b, c: (row(b, c), 0)),
            pl.BlockSpec((None, CONV_WIDTH, conv_dim), lambda b, c: (layer, 0, 0)),
            par(conv_dim), par(LANES), par(d_ssm), par(d_ssm),
        ],
        out_specs=pl.BlockSpec((SSD_CHUNK, d_ssm), lambda b, c: (row(b, c), 0)),
        out_shape=jax.ShapeDtypeStruct((t, d_ssm), BF16),
        scratch_shapes=[
            pltpu.VMEM((SSD_CHUNK + SUBLANES, conv_dim), F32),
            pltpu.VMEM((SSM_GROUPS, SSM_STATE, d_ssm // SSM_GROUPS), F32),
            pltpu.VMEM((SSD_CHUNK, d_ssm), F32),
        ],
        compiler_params=_params("arbitrary", "arbitrary"),
        name="ssd_heads",
    )(proj, proj, proj, dt_sp, conv_w, conv_b, a_log_pad, d_skip_x, ssm_norm)


def _attn_kernel(q_ref, k_ref, v_ref, cum_ref, gn_ref, o_ref, qa_ref, ka_ref, vt_ref, p_ref):
    h = pl.program_id(1)
    seq = q_ref.shape[0]
    tq = p_ref.shape[1]
    lane = lax.broadcasted_iota(jnp.int32, (seq, LANES), 1)
    c2 = LOG2E * jnp.sum(jnp.where(lane == 2 * SUBLANES + h, cum_ref[...], 0.0), axis=1, keepdims=True)
    hi, mid, lo = (piece.astype(F32) for piece in _split3(c2))
    aug_q = jnp.where(lane == 0, hi, jnp.where(lane == 1, mid, jnp.where(lane == 2, lo,
                      jnp.where(lane < 6, 1.0, 0.0))))
    aug_k = jnp.where(lane < 3, 1.0, jnp.where(lane == 3, -hi, jnp.where(lane == 4, -mid,
                      jnp.where(lane == 5, -lo, 0.0))))
    qa_ref[:, 0:LANES] = (q_ref[...] * (LOG2E * ATTN_HEAD_DIM ** -0.5)).astype(BF16)
    qa_ref[:, LANES:] = aug_q.astype(BF16)
    ka_ref[:, 0:LANES] = k_ref[...].astype(BF16)
    ka_ref[:, LANES:] = aug_k.astype(BF16)
    vt_ref[...] = v_ref[...].T.astype(BF16)

    srow = lax.broadcasted_iota(jnp.int32, (tq, tq), 0)
    tcol = lax.broadcasted_iota(jnp.int32, (tq, tq), 1)
    causal = srow <= tcol
    nt = (((1,), (1,)), ((), ()))
    for i in range(seq // tq):
        lo_k, hi_k = i * tq, (i + 1) * tq
        qa = qa_ref[lo_k:hi_k, :]
        s_diag = jnp.where(causal, lax.dot_general(ka_ref[lo_k:hi_k, :], qa, nt, preferred_element_type=F32),
                           -jnp.inf)
        m = jnp.max(s_diag, axis=0, keepdims=True)
        if i > 0:
            s_past = lax.dot_general(ka_ref[0:lo_k, :], qa, nt, preferred_element_type=F32)
            m = jnp.maximum(m, jnp.max(s_past, axis=0, keepdims=True))
            p_past = jnp.exp2(s_past - m)
            p_ref[0:lo_k, :] = p_past.astype(BF16)
        p_diag = jnp.exp2(s_diag - m)
        p_ref[lo_k:hi_k, :] = p_diag.astype(BF16)
        l = jnp.sum(p_diag, axis=0, keepdims=True)
        if i > 0:
            l = l + jnp.sum(p_past, axis=0, keepdims=True)
        o_t = _dot(vt_ref[:, 0:hi_k], p_ref[0:hi_k, :]) / l
        o_t = o_t * lax.rsqrt(jnp.mean(o_t * o_t, axis=0, keepdims=True) + EPS)
        o_ref[lo_k:hi_k, :] = (o_t.T * gn_ref[...]).astype(o_ref.dtype)


def _attention(proj, cum, attn_norm, layer, batch, seq, q_blk, k_blk, v_blk):
    t = proj.shape[0]
    d_attn = ATTN_HEADS * ATTN_HEAD_DIM
    col = lambda blk: pl.BlockSpec((seq, ATTN_HEAD_DIM), lambda b, h: (b, blk + h))
    return pl.pallas_call(
        _attn_kernel,
        grid=(batch, ATTN_HEADS),
        in_specs=[
            col(q_blk), col(k_blk), col(v_blk),
            pl.BlockSpec((seq, LANES), lambda b, h: (b, 0)),
            pl.BlockSpec((None, 1, ATTN_HEAD_DIM), lambda b, h: (layer, 0, h)),
        ],
        out_specs=pl.BlockSpec((seq, ATTN_HEAD_DIM), lambda b, h: (b, h)),
        out_shape=jax.ShapeDtypeStruct((t, d_attn), BF16),
        scratch_shapes=[
            pltpu.VMEM((seq, 2 * LANES), BF16),
            pltpu.VMEM((seq, 2 * LANES), BF16),
            pltpu.VMEM((ATTN_HEAD_DIM, seq), BF16),
            pltpu.VMEM((seq, ATTN_TQ), BF16),
        ],
        compiler_params=_params("arbitrary", "arbitrary"),
        name="forgetting_attention",
    )(proj, proj, proj, cum, attn_norm)


def _out_kernel(x_ref, ys_ref, ya_ref, w_ref, gate_ref, gpost_ref, o_ref):
    d_ssm = ys_ref.shape[1]
    h = _dot(ys_ref[...], w_ref[0:d_ssm, :]) + _dot(ya_ref[...], w_ref[d_ssm:, :])
    o_ref[...] = x_ref[...] + gate_ref[...] * _rms(h, gpost_ref[...])


def _out_proj(x, y_ssm, y_attn, w_out, mod, norm_post, layer, seq, mod_rows):
    t, d = x.shape
    per_b = seq // OUT_TM
    spec1 = lambda fn: pl.BlockSpec((None, 1, d), lambda i: (fn(i), 0, 0))
    return pl.pallas_call(
        _out_kernel,
        grid=(t // OUT_TM,),
        in_specs=[
            pl.BlockSpec((OUT_TM, d), lambda i: (i, 0)),
            pl.BlockSpec((OUT_TM, y_ssm.shape[1]), lambda i: (i, 0)),
            pl.BlockSpec((OUT_TM, y_attn.shape[1]), lambda i: (i, 0)),
            pl.BlockSpec((None,) + w_out.shape[1:], lambda i: (layer, 0, 0)),
            spec1(lambda i: (layer * mod_rows + i // per_b) * N_MOD + 5),
            spec1(lambda i: layer * 3 + 1),
        ],
        out_specs=pl.BlockSpec((OUT_TM, d), lambda i: (i, 0)),
        out_shape=jax.ShapeDtypeStruct((t, d), F32),
        compiler_params=_params("arbitrary"),
        name="mixer_out_proj",
    )(x, y_ssm, y_attn, w_out, mod, norm_post)


def kernel(x, c, norm_pre, norm_post, w_mod, b_mod, w_ffn_gate, w_ffn_up, w_ffn_down, w_in, conv_w,
           conv_b, dt_bias, a_log, d_skip, ssm_norm, f_bias, attn_norm, w_out):
    batch, seq, d = x.shape
    depth = w_mod.shape[0]
    d_ssm = SSM_HEADS * SSM_HEAD_DIM
    d_attn = ATTN_HEADS * ATTN_HEAD_DIM
    conv_dim = d_ssm + 2 * SSM_GROUPS * SSM_STATE
    assert seq % max(FFN_TM, PROJ_TM, OUT_TM, ATTN_TQ, SSD_CHUNK) == 0
    assert w_in.shape[-1] == 2 * d_ssm + 2 * SSM_GROUPS * SSM_STATE + SSM_HEADS + 3 * d_attn + ATTN_HEADS

    wg, wu, wd = (w.astype(BF16) for w in (w_ffn_gate, w_ffn_up, w_ffn_down))
    wo = w_out.astype(BF16)
    o_dt = d_ssm + conv_dim
    o_q = o_dt + SSM_HEADS
    o_f = o_q + 3 * d_attn
    pad = LANES - SSM_HEADS - ATTN_HEADS
    w_in_groups = (
        w_in[..., :o_dt].astype(BF16),
        w_in[..., o_q:o_f].astype(BF16),
        jnp.concatenate([w_in[..., o_dt:o_q], w_in[..., o_f:],
                         jnp.zeros(w_in.shape[:-1] + (pad,), w_in.dtype)], axis=-1).astype(BF16),
    )
    q_blk, k_blk, v_blk = 0, d_attn // LANES, 2 * d_attn // LANES

    gate_bias = jnp.concatenate([dt_bias, f_bias, jnp.zeros((depth, pad), F32)], axis=-1)[:, None, :]
    a_log_pad = jnp.pad(a_log, ((0, 0), (0, LANES - SSM_HEADS)))[:, None, :]
    d_skip_x = jnp.repeat(d_skip, SSM_HEAD_DIM, axis=-1)[:, None, :]
    conv_b3 = conv_b[:, None, :]
    ssm_norm3 = ssm_norm[:, None, :]
    attn_norm3 = attn_norm[:, None, :]
    norm_pre_t = norm_pre.reshape(depth * 3, 1, d)
    norm_post_t = norm_post.reshape(depth * 3, 1, d)

    mod_rows = 2 * SUBLANES
    c_pad = jnp.pad(c, ((0, mod_rows - batch), (0, 0)))
    mod = _modulation(c_pad, w_mod, b_mod).reshape(depth * mod_rows * N_MOD, 1, d)

    xt = x.reshape(batch * seq, d)
    for l in range(depth):
        xt = _ffn(xt, mod, norm_pre_t, norm_post_t, wg, wu, wd, l, 0, 0, seq, mod_rows)
        zx, qkv, gt = _in_proj(xt, mod, norm_pre_t, w_in_groups, l, seq, mod_rows)
        dt_sp, cum = _gates(gt, gate_bias, l, batch, seq)
        y_ssm = _ssd(zx, dt_sp, conv_w, conv_b3, a_log_pad, d_skip_x, ssm_norm3, l, batch, seq)
        y_attn = _attention(qkv, cum, attn_norm3, l, batch, seq, q_blk, k_blk, v_blk)
        xt = _out_proj(xt, y_ssm, y_attn, wo, mod, norm_post_t, l, seq, mod_rows)
        xt = _ffn(xt, mod, norm_pre_t, norm_post_t, wg, wu, wd, l, 1, 2, seq, mod_rows)
    return xt.reshape(batch, seq, d)
```

```python
import functools

import jax
import jax.numpy as jnp
from jax import lax
from jax.experimental import pallas as pl
from jax.experimental.pallas import tpu as pltpu

F32 = jnp.float32
BF16 = jnp.bfloat16

EPS = 1e-6
N_MOD = 9
SSM_HEAD_DIM = 64
SSM_HEADS = 16
SSM_GROUPS = 2
SSM_STATE = 128
CONV_WIDTH = 4
ATTN_HEAD_DIM = 128
ATTN_HEADS = 8
LANES = 128
SUBLANES = 8
VMEM_LIMIT_BYTES = 56 * 1024 * 1024

FFN_TM, FFN_TF, FFN_TN = 512, 512, 512
ROW_CHUNK, ROW_UNROLL = 16, 4
PROJ_TM = 256
OUT_TM = 512
MOD_TN = 1024
SPLIT_ROWS = 256
SSD_CHUNK = 128
ATTN_TQ = 512
LOG2E = 1.4426950408889634
GATE_BLK = 128


def _params(*sem):
    return pltpu.CompilerParams(dimension_semantics=sem, vmem_limit_bytes=VMEM_LIMIT_BYTES)


def _silu(x):
    return x / (1.0 + jnp.exp(-x))


def _rms(x, g):
    return x * lax.rsqrt(jnp.mean(x * x, axis=-1, keepdims=True) + EPS) * g


def _norm_mod(x, g, shift, scale):
    return _rms(x, g) * (1.0 + scale) + shift


def _split3(x):
    hi = x.astype(BF16)
    r = x - hi.astype(F32)
    mid = r.astype(BF16)
    lo = (r - mid.astype(F32)).astype(BF16)
    return hi, mid, lo


def _dot(a, b):
    return jnp.dot(a, b, preferred_element_type=F32)


def _dot3(m01, x):
    hi, mid, lo = _split3(x)
    return _dot(m01, hi) + _dot(m01, mid) + _dot(m01, lo)


def _dot3r(x, m01):
    hi, mid, lo = _split3(x)
    return _dot(hi, m01) + _dot(mid, m01) + _dot(lo, m01)


def _tril_ones(n):
    r = lax.broadcasted_iota(jnp.int32, (n, n), 0)
    c = lax.broadcasted_iota(jnp.int32, (n, n), 1)
    return r >= c


def _mod_kernel(c_ref, w_ref, b_ref, o_ref):
    ca = _silu(c_ref[...]).astype(BF16)
    o_ref[...] = _dot(ca, w_ref[...].astype(BF16)) + b_ref[...]


def _modulation(c_pad, w_mod, b_mod):
    depth, d, n = w_mod.shape
    rows = c_pad.shape[0]
    return pl.pallas_call(
        _mod_kernel,
        grid=(depth, n // MOD_TN),
        in_specs=[
            pl.BlockSpec((rows, d), lambda l, j: (0, 0)),
            pl.BlockSpec((None, d, MOD_TN), lambda l, j: (l, 0, j)),
            pl.BlockSpec((None, 1, MOD_TN), lambda l, j: (l, 0, j)),
        ],
        out_specs=pl.BlockSpec((None, rows, MOD_TN), lambda l, j: (l, 0, j)),
        out_shape=jax.ShapeDtypeStruct((depth, rows, n), F32),
        compiler_params=_params("arbitrary", "arbitrary"),
        name="modulation",
    )(c_pad, w_mod, b_mod.reshape(depth, 1, n))


def _norm_mod_rows(x_ref, g_ref, sh_ref, sc_ref, gs_ref, h_ref):
    gs_ref[...] = g_ref[...] * (1.0 + sc_ref[...])

    def body(r, carry):
        rows = pl.ds(pl.multiple_of(r * ROW_CHUNK, ROW_CHUNK), ROW_CHUNK)
        xv = x_ref[rows, :]
        rs = lax.rsqrt(jnp.mean(xv * xv, axis=-1, keepdims=True) + EPS)
        h_ref[rows, :] = (xv * rs * gs_ref[...] + sh_ref[...]).astype(BF16)
        return carry

    lax.fori_loop(0, x_ref.shape[0] // ROW_CHUNK, body, 0, unroll=ROW_UNROLL)


def _ffn_kernel(x_ref, gpre_ref, sh_ref, sc_ref, gate_ref, gpost_ref, wg_ref, wu_ref, wd_ref, *rest,
                cast_next):
    if cast_next:
        nwg_ref, nwu_ref, nwd_ref, o_ref, cwg_ref, cwu_ref, cwd_ref, h_ref, a_ref, y_ref, gs_ref = rest
    else:
        o_ref, h_ref, a_ref, y_ref, gs_ref = rest
    s = pl.program_id(1)
    nf, tm, _ = a_ref.shape
    nd, _, tn = y_ref.shape
    d = x_ref.shape[1]

    @pl.when(s == 0)
    def _():
        _norm_mod_rows(x_ref, gpre_ref, sh_ref, sc_ref, gs_ref, h_ref)

    @pl.when(s < nf)
    def _():
        h = h_ref[...]
        g = _dot(h, wg_ref[...])
        u = _dot(h, wu_ref[...])
        a_ref[s] = (_silu(g) * u).astype(BF16)
        if cast_next:
            cwd_ref[...] = nwd_ref[...].astype(BF16)

    @pl.when(s >= nf)
    def _():
        a = jnp.concatenate([a_ref[k] for k in range(nf)], axis=1)
        y_ref[s - nf] = _dot(a, wd_ref[...])
        if cast_next:
            cwg_ref[...] = nwg_ref[...].astype(BF16)
            cwu_ref[...] = nwu_ref[...].astype(BF16)

    @pl.when(s == nf + nd - 1)
    def _():
        gs_ref[...] = (0.5 * gate_ref[...]) * gpost_ref[...]

        def body(r, carry):
            rows = pl.ds(pl.multiple_of(r * ROW_CHUNK, ROW_CHUNK), ROW_CHUNK)
            ys = [y_ref[n, rows, :] for n in range(nd)]
            ss = sum(jnp.sum(y * y, axis=-1, keepdims=True) for y in ys)
            rs = lax.rsqrt(ss * (1.0 / d) + EPS)
            for n in range(nd):
                cols = slice(n * tn, (n + 1) * tn)
                o_ref[rows, cols] = x_ref[rows, cols] + ys[n] * rs * gs_ref[:, cols]
            return carry

        lax.fori_loop(0, tm // ROW_CHUNK, body, 0, unroll=ROW_UNROLL)


def _vec_spec(row_fn):
    return lambda d: pl.BlockSpec((None, 1, d), lambda i, j: (row_fn(i), 0, 0))


def _ffn(x, mod, norm_pre, norm_post, weights, next_weights, layer, sub, seq, mod_rows):
    wg, wu, wd = weights
    t, d = x.shape
    dff = wg.shape[-1]
    nf, nd = dff // FFN_TF, d // FFN_TN
    nblk = t // FFN_TM
    per_b = seq // FFN_TM
    mrow = lambda j: (lambda i: (layer * mod_rows + i // per_b) * N_MOD + j)
    nrow = lambda i: layer * 3 + sub
    spec = lambda fn: _vec_spec(fn)(d)
    up_tile = lambda i, s: (0, jnp.minimum(s, nf - 1))
    in_specs = [
        pl.BlockSpec((FFN_TM, d), lambda i, s: (i, 0)),
        spec(nrow), spec(mrow(3 * sub)), spec(mrow(3 * sub + 1)), spec(mrow(3 * sub + 2)),
        spec(nrow),
        pl.BlockSpec((d, FFN_TF), up_tile),
        pl.BlockSpec((d, FFN_TF), up_tile),
        pl.BlockSpec((dff, FFN_TN), lambda i, s: (0, jnp.maximum(s - nf, 0))),
    ]
    out_specs = [pl.BlockSpec((FFN_TM, d), lambda i, s: (i, 0))]
    out_shape = [jax.ShapeDtypeStruct((t, d), F32)]
    operands = [x, norm_pre, mod, mod, mod, norm_post, wg, wu, wd]
    if next_weights is not None:
        (nwg, nwu, nwd), nl, ns = next_weights
        up_rows, dn_rows = d // (nblk * nd), dff // (nblk * nf)
        assert up_rows % (2 * SUBLANES) == 0 and dn_rows % (2 * SUBLANES) == 0
        up_slab = lambda i, s: i * nd + jnp.clip(s - nf, 0, nd - 1)
        dn_slab = lambda i, s: i * nf + jnp.minimum(s, nf - 1)
        in_specs += [
            pl.BlockSpec((None, None, up_rows, dff), lambda i, s: (nl, ns, up_slab(i, s), 0)),
            pl.BlockSpec((None, None, up_rows, dff), lambda i, s: (nl, ns, up_slab(i, s), 0)),
            pl.BlockSpec((None, None, dn_rows, d), lambda i, s: (nl, ns, dn_slab(i, s), 0)),
        ]
        out_specs += [
            pl.BlockSpec((up_rows, dff), lambda i, s: (up_slab(i, s), 0)),
            pl.BlockSpec((up_rows, dff), lambda i, s: (up_slab(i, s), 0)),
            pl.BlockSpec((dn_rows, d), lambda i, s: (dn_slab(i, s), 0)),
        ]
        out_shape += [jax.ShapeDtypeStruct((d, dff), BF16), jax.ShapeDtypeStruct((d, dff), BF16),
                      jax.ShapeDtypeStruct((dff, d), BF16)]
        operands += [nwg, nwu, nwd]
    outs = pl.pallas_call(
        functools.partial(_ffn_kernel, cast_next=next_weights is not None),
        grid=(nblk, nf + nd),
        in_specs=in_specs,
        out_specs=out_specs,
        out_shape=out_shape,
        scratch_shapes=[
            pltpu.VMEM((FFN_TM, d), BF16),
            pltpu.VMEM((nf, FFN_TM, FFN_TF), BF16),
            pltpu.VMEM((nd, FFN_TM, FFN_TN), F32),
            pltpu.VMEM((1, d), F32),
        ],
        compiler_params=_params("arbitrary", "arbitrary"),
        name="swiglu_half_step",
    )(*operands)
    return outs[0], tuple(outs[1:])


def _split_w_in(w_in, bounds):
    depth, d, n = w_in.shape

    def body(w_ref, *o_refs):
        for (lo, hi), o_ref in zip(bounds, o_refs):
            o_ref[...] = w_ref[:, lo:hi].astype(BF16)

    return pl.pallas_call(
        body,
        grid=(depth, d // SPLIT_ROWS),
        in_specs=[pl.BlockSpec((None, SPLIT_ROWS, n), lambda l, r: (l, r, 0))],
        out_specs=[pl.BlockSpec((None, SPLIT_ROWS, hi - lo), lambda l, r: (l, r, 0)) for lo, hi in bounds],
        out_shape=[jax.ShapeDtypeStruct((depth, d, hi - lo), BF16) for lo, hi in bounds],
        compiler_params=_params("arbitrary", "arbitrary"),
        name="split_w_in",
    )(w_in)


def _in_proj(x, mod, norm_pre, weights, layer, seq, mod_rows):
    t, d = x.shape
    per_b = seq // PROJ_TM
    row1 = lambda fn: pl.BlockSpec((None, 1, d), lambda i: (fn(i), 0, 0))
    mrow = lambda j: (lambda i: (layer * mod_rows + i // per_b) * N_MOD + j)
    nw = len(weights)

    def body(x_ref, gpre_ref, sh_ref, sc_ref, *rest):
        w_refs, o_refs, (h_ref, gs_ref) = rest[:nw], rest[nw:2 * nw], rest[2 * nw:]
        _norm_mod_rows(x_ref, gpre_ref, sh_ref, sc_ref, gs_ref, h_ref)
        h = h_ref[...]
        for w_ref, o_ref in zip(w_refs, o_refs):
            o_ref[...] = _dot(h, w_ref[...])

    return pl.pallas_call(
        body,
        grid=(t // PROJ_TM,),
        in_specs=[
            pl.BlockSpec((PROJ_TM, d), lambda i: (i, 0)),
            row1(lambda i: layer * 3 + 1), row1(mrow(3)), row1(mrow(4)),
        ] + [pl.BlockSpec((None,) + w.shape[1:], lambda i: (layer, 0, 0), pipeline_mode=pl.Buffered(1))
             for w in weights],
        out_specs=[pl.BlockSpec((PROJ_TM, w.shape[-1]), lambda i: (i, 0)) for w in weights],
        out_shape=[jax.ShapeDtypeStruct((t, w.shape[-1]), F32) for w in weights],
        scratch_shapes=[pltpu.VMEM((PROJ_TM, d), BF16), pltpu.VMEM((1, d), F32)],
        compiler_params=_params("arbitrary"),
        name="mixer_in_proj",
    )(x, norm_pre, mod, mod, *weights)


def _gates_kernel(p_ref, bias_ref, dt_ref, cum_ref):
    seq = p_ref.shape[0]
    tril = _tril_ones(GATE_BLK).astype(BF16)
    carry = jnp.zeros((1, LANES), F32)
    for blk in range(seq // GATE_BLK):
        rows = pl.ds(blk * GATE_BLK, GATE_BLK)
        u = p_ref[rows, :] + bias_ref[...]
        t = jnp.log1p(jnp.exp(-jnp.abs(u)))
        dt_ref[rows, :] = jnp.maximum(u, 0.0) + t
        log_f = jnp.minimum(u, 0.0) - t
        cs = _dot3(tril, log_f) + carry
        cum_ref[rows, :] = cs
        carry = cs[GATE_BLK - 1:GATE_BLK, :]


def _gates(proj, gate_bias, layer, batch, seq):
    t = proj.shape[0]
    return pl.pallas_call(
        _gates_kernel,
        grid=(batch,),
        in_specs=[
            pl.BlockSpec((seq, LANES), lambda b: (b, 0)),
            pl.BlockSpec((None, 1, LANES), lambda b: (layer, 0, 0)),
        ],
        out_specs=[
            pl.BlockSpec((seq, LANES), lambda b: (b, 0)),
            pl.BlockSpec((seq, LANES), lambda b: (b, 0)),
        ],
        out_shape=[
            jax.ShapeDtypeStruct((t, LANES), F32),
            jax.ShapeDtypeStruct((t, LANES), F32),
        ],
        compiler_params=_params("arbitrary"),
        name="mixer_gates",
    )(proj, gate_bias)


def _ssd_kernel(xs_ref, bc_ref, z_ref, dt_ref, cw_ref, cb_ref, alog_ref, dskip_ref, gn_ref,
                o_ref, ubuf_ref, state_ref, ydiag_ref):
    lc = xs_ref.shape[0]
    d_ssm = xs_ref.shape[1]
    gw = d_ssm // SSM_GROUPS
    hpg = SSM_HEADS // SSM_GROUPS

    @pl.when(pl.program_id(1) == 0)
    def _():
        ubuf_ref[0:SUBLANES, :] = jnp.zeros((SUBLANES, ubuf_ref.shape[1]), F32)
        state_ref[...] = jnp.zeros(state_ref.shape, F32)

    ubuf_ref[SUBLANES:, 0:d_ssm] = xs_ref[...]
    ubuf_ref[SUBLANES:, d_ssm:] = bc_ref[...]
    u_ext = ubuf_ref[...]
    acc = cw_ref[0:1, :] * u_ext
    for k in range(1, CONV_WIDTH):
        acc = pltpu.roll(acc, 1, axis=0) + cw_ref[k:k + 1, :] * u_ext
    ubuf_ref[0:SUBLANES, :] = ubuf_ref[lc:lc + SUBLANES, :]
    xbc = _silu(acc[SUBLANES:, :] + cb_ref[...])
    xs = xbc[:, 0:d_ssm]
    b_mat = xbc[:, d_ssm:d_ssm + SSM_GROUPS * SSM_STATE].astype(BF16)
    c_mat = xbc[:, d_ssm + SSM_GROUPS * SSM_STATE:].astype(BF16)

    lane = lax.broadcasted_iota(jnp.int32, (1, LANES), 1)
    a = jnp.where(lane < SSM_HEADS, -jnp.exp(alog_ref[...]), 0.0)
    dt = dt_ref[...]
    tri = _tril_ones(lc)
    a_cs = _dot3(tri.astype(BF16), dt * a)
    a_cs_t = a_cs.T
    a_last = a_cs[lc - 1:lc, :]

    er = lax.broadcasted_iota(jnp.int32, (LANES, d_ssm), 0)
    ec = lax.broadcasted_iota(jnp.int32, (LANES, d_ssm), 1)
    expand = (ec // SSM_HEAD_DIM == er).astype(BF16)
    dt_x = _dot3r(dt, expand)
    dec_in = _dot3r(jnp.exp(a_cs), expand)
    dec_out = _dot3r(jnp.exp(a_last - a_cs), expand)
    xdt = xs * dt_x

    for g in range(SSM_GROUPS):
        cg = c_mat[:, g * SSM_STATE:(g + 1) * SSM_STATE]
        bg = b_mat[:, g * SSM_STATE:(g + 1) * SSM_STATE]
        cb = lax.dot_general(cg, bg, (((1,), (1,)), ((), ())), preferred_element_type=F32)
        for hh in range(hpg):
            h = g * hpg + hh
            seg = a_cs[:, h:h + 1] - a_cs_t[h:h + 1, :]
            m = (cb * jnp.exp(jnp.where(tri, seg, -jnp.inf))).astype(BF16)
            cols = slice(h * SSM_HEAD_DIM, (h + 1) * SSM_HEAD_DIM)
            ydiag_ref[:, cols] = _dot(m, xdt[:, cols].astype(BF16))

    xw = (xdt * dec_out).astype(BF16)
    y_parts = []
    for g in range(SSM_GROUPS):
        cols = slice(g * gw, (g + 1) * gw)
        cg = c_mat[:, g * SSM_STATE:(g + 1) * SSM_STATE]
        bg = b_mat[:, g * SSM_STATE:(g + 1) * SSM_STATE]
        st = state_ref[g]
        y_parts.append(_dot(cg, st.astype(BF16)))
        upd = lax.dot_general(bg, xw[:, cols], (((0,), (0,)), ((), ())), preferred_element_type=F32)
        state_ref[g] = st * dec_in[lc - 1:lc, cols] + upd
    y_off = jnp.concatenate(y_parts, axis=1) * dec_in

    y = (ydiag_ref[...] + y_off + dskip_ref[...] * xs) * _silu(z_ref[...])
    outs = []
    for g in range(SSM_GROUPS):
        cols = slice(g * gw, (g + 1) * gw)
        outs.append(_rms(y[:, cols], gn_ref[:, cols]))
    o_ref[...] = jnp.concatenate(outs, axis=1).astype(o_ref.dtype)


def _ssd(proj, dt_sp, conv_w, conv_b, a_log_pad, d_skip_x, ssm_norm, layer, batch, seq):
    t = proj.shape[0]
    d_ssm = SSM_HEADS * SSM_HEAD_DIM
    d_bc = 2 * SSM_GROUPS * SSM_STATE
    conv_dim = d_ssm + d_bc
    nc = seq // SSD_CHUNK
    row = lambda b, c: b * nc + c
    par = lambda width: pl.BlockSpec((None, 1, width), lambda b, c: (layer, 0, 0))
    return pl.pallas_call(
        _ssd_kernel,
        grid=(batch, nc),
        in_specs=[
            pl.BlockSpec((SSD_CHUNK, d_ssm), lambda b, c: (row(b, c), 1)),
            pl.BlockSpec((SSD_CHUNK, d_bc), lambda b, c: (row(b, c), 2 * d_ssm // d_bc)),
            pl.BlockSpec((SSD_CHUNK, d_ssm), lambda b, c: (row(b, c), 0)),
            pl.BlockSpec((SSD_CHUNK, LANES), lambda b, c: (row(b, c), 0)),
            pl.BlockSpec((None, CONV_WIDTH, conv_dim), lambda b, c: (layer, 0, 0)),
            par(conv_dim), par(LANES), par(d_ssm), par(d_ssm),
        ],
        out_specs=pl.BlockSpec((SSD_CHUNK, d_ssm), lambda b, c: (row(b, c), 0)),
        out_shape=jax.ShapeDtypeStruct((t, d_ssm), BF16),
        scratch_shapes=[
            pltpu.VMEM((SSD_CHUNK + SUBLANES, conv_dim), F32),
            pltpu.VMEM((SSM_GROUPS, SSM_STATE, d_ssm // SSM_GROUPS), F32),
            pltpu.VMEM((SSD_CHUNK, d_ssm), F32),
        ],
        compiler_params=_params("arbitrary", "arbitrary"),
        name="ssd_heads",
    )(proj, proj, proj, dt_sp, conv_w, conv_b, a_log_pad, d_skip_x, ssm_norm)


def _attn_kernel(q_ref, k_ref, v_ref, cum_ref, gn_ref, o_ref, qa_ref, ka_ref, vt_ref, p_ref):
    h = pl.program_id(1)
    seq = q_ref.shape[0]
    tq = p_ref.shape[1]
    lane = lax.broadcasted_iota(jnp.int32, (seq, LANES), 1)
    c2 = LOG2E * jnp.sum(jnp.where(lane == 2 * SUBLANES + h, cum_ref[...], 0.0), axis=1, keepdims=True)
    hi, mid, lo = (piece.astype(F32) for piece in _split3(c2))
    aug_q = jnp.where(lane == 0, hi, jnp.where(lane == 1, mid, jnp.where(lane == 2, lo,
                      jnp.where(lane < 6, 1.0, 0.0))))
    aug_k = jnp.where(lane < 3, 1.0, jnp.where(lane == 3, -hi, jnp.where(lane == 4, -mid,
                      jnp.where(lane == 5, -lo, 0.0))))
    qa_ref[:, 0:LANES] = (q_ref[...] * (LOG2E * ATTN_HEAD_DIM ** -0.5)).astype(BF16)
    qa_ref[:, LANES:] = aug_q.astype(BF16)
    ka_ref[:, 0:LANES] = k_ref[...].astype(BF16)
    ka_ref[:, LANES:] = aug_k.astype(BF16)
    vt_ref[...] = v_ref[...].T.astype(BF16)

    srow = lax.broadcasted_iota(jnp.int32, (tq, tq), 0)
    tcol = lax.broadcasted_iota(jnp.int32, (tq, tq), 1)
    causal = srow <= tcol
    nt = (((1,), (1,)), ((), ()))
    for i in range(seq // tq):
        lo_k, hi_k = i * tq, (i + 1) * tq
        qa = qa_ref[lo_k:hi_k, :]
        s_diag = jnp.where(causal, lax.dot_general(ka_ref[lo_k:hi_k, :], qa, nt, preferred_element_type=F32),
                           -jnp.inf)
        m = jnp.max(s_diag, axis=0, keepdims=True)
        if i > 0:
            s_past = lax.dot_general(ka_ref[0:lo_k, :], qa, nt, preferred_element_type=F32)
            m = jnp.maximum(m, jnp.max(s_past, axis=0, keepdims=True))
            p_past = jnp.exp2(s_past - m)
            p_ref[0:lo_k, :] = p_past.astype(BF16)
        p_diag = jnp.exp2(s_diag - m)
        p_ref[lo_k:hi_k, :] = p_diag.astype(BF16)
        l = jnp.sum(p_diag, axis=0, keepdims=True)
        if i > 0:
            l = l + jnp.sum(p_past, axis=0, keepdims=True)
        o_t = _dot(vt_ref[:, 0:hi_k], p_ref[0:hi_k, :]) / l
        o_t = o_t * lax.rsqrt(jnp.mean(o_t * o_t, axis=0, keepdims=True) + EPS)
        o_ref[lo_k:hi_k, :] = (o_t.T * gn_ref[...]).astype(o_ref.dtype)


def _attention(proj, cum, attn_norm, layer, batch, seq, q_blk, k_blk, v_blk):
    t = proj.shape[0]
    d_attn = ATTN_HEADS * ATTN_HEAD_DIM
    col = lambda blk: pl.BlockSpec((seq, ATTN_HEAD_DIM), lambda b, h: (b, blk + h))
    return pl.pallas_call(
        _attn_kernel,
        grid=(batch, ATTN_HEADS),
        in_specs=[
            col(q_blk), col(k_blk), col(v_blk),
            pl.BlockSpec((seq, LANES), lambda b, h: (b, 0)),
            pl.BlockSpec((None, 1, ATTN_HEAD_DIM), lambda b, h: (layer, 0, h)),
        ],
        out_specs=pl.BlockSpec((seq, ATTN_HEAD_DIM), lambda b, h: (b, h)),
        out_shape=jax.ShapeDtypeStruct((t, d_attn), BF16),
        scratch_shapes=[
            pltpu.VMEM((seq, 2 * LANES), BF16),
            pltpu.VMEM((seq, 2 * LANES), BF16),
            pltpu.VMEM((ATTN_HEAD_DIM, seq), BF16),
            pltpu.VMEM((seq, ATTN_TQ), BF16),
        ],
        compiler_params=_params("arbitrary", "arbitrary"),
        name="forgetting_attention",
    )(proj, proj, proj, cum, attn_norm)


def _out_kernel(x_ref, ys_ref, ya_ref, w_ref, gate_ref, gpost_ref, o_ref):
    d_ssm = ys_ref.shape[1]
    h = _dot(ys_ref[...], w_ref[0:d_ssm, :]) + _dot(ya_ref[...], w_ref[d_ssm:, :])
    o_ref[...] = x_ref[...] + gate_ref[...] * _rms(h, gpost_ref[...])


def _out_proj(x, y_ssm, y_attn, w_out, mod, norm_post, layer, seq, mod_rows):
    t, d = x.shape
    per_b = seq // OUT_TM
    spec1 = lambda fn: pl.BlockSpec((None, 1, d), lambda i: (fn(i), 0, 0))
    return pl.pallas_call(
        _out_kernel,
        grid=(t // OUT_TM,),
        in_specs=[
            pl.BlockSpec((OUT_TM, d), lambda i: (i, 0)),
            pl.BlockSpec((OUT_TM, y_ssm.shape[1]), lambda i: (i, 0)),
            pl.BlockSpec((OUT_TM, y_attn.shape[1]), lambda i: (i, 0)),
            pl.BlockSpec((None,) + w_out.shape[1:], lambda i: (layer, 0, 0)),
            spec1(lambda i: (layer * mod_rows + i // per_b) * N_MOD + 5),
            spec1(lambda i: layer * 3 + 1),
        ],
        out_specs=pl.BlockSpec((OUT_TM, d), lambda i: (i, 0)),
        out_shape=jax.ShapeDtypeStruct((t, d), F32),
        compiler_params=_params("arbitrary"),
        name="mixer_out_proj",
    )(x, y_ssm, y_attn, w_out, mod, norm_post)


def kernel(x, c, norm_pre, norm_post, w_mod, b_mod, w_ffn_gate, w_ffn_up, w_ffn_down, w_in, conv_w,
           conv_b, dt_bias, a_log, d_skip, ssm_norm, f_bias, attn_norm, w_out):
    batch, seq, d = x.shape
    depth = w_mod.shape[0]
    d_ssm = SSM_HEADS * SSM_HEAD_DIM
    d_attn = ATTN_HEADS * ATTN_HEAD_DIM
    conv_dim = d_ssm + 2 * SSM_GROUPS * SSM_STATE
    assert seq % max(FFN_TM, PROJ_TM, OUT_TM, ATTN_TQ, SSD_CHUNK) == 0
    assert w_in.shape[-1] == 2 * d_ssm + 2 * SSM_GROUPS * SSM_STATE + SSM_HEADS + 3 * d_attn + ATTN_HEADS

    ffn_f32 = (w_ffn_gate, w_ffn_up, w_ffn_down)
    ffn_w = tuple(w[0, 0].astype(BF16) for w in ffn_f32)
    wo = w_out.astype(BF16)
    o_dt = d_ssm + conv_dim
    o_q = o_dt + SSM_HEADS
    o_f = o_q + 3 * d_attn
    pad = LANES - SSM_HEADS - ATTN_HEADS
    w_in_groups = (
        *_split_w_in(w_in, ((0, o_dt), (o_q, o_f))),
        jnp.concatenate([w_in[..., o_dt:o_q], w_in[..., o_f:],
                         jnp.zeros(w_in.shape[:-1] + (pad,), w_in.dtype)], axis=-1).astype(BF16),
    )
    q_blk, k_blk, v_blk = 0, d_attn // LANES, 2 * d_attn // LANES

    gate_bias = jnp.concatenate([dt_bias, f_bias, jnp.zeros((depth, pad), F32)], axis=-1)[:, None, :]
    a_log_pad = jnp.pad(a_log, ((0, 0), (0, LANES - SSM_HEADS)))[:, None, :]
    d_skip_x = jnp.repeat(d_skip, SSM_HEAD_DIM, axis=-1)[:, None, :]
    conv_b3 = conv_b[:, None, :]
    ssm_norm3 = ssm_norm[:, None, :]
    attn_norm3 = attn_norm[:, None, :]
    norm_pre_t = norm_pre.reshape(depth * 3, 1, d)
    norm_post_t = norm_post.reshape(depth * 3, 1, d)

    mod_rows = 2 * SUBLANES
    c_pad = jnp.pad(c, ((0, mod_rows - batch), (0, 0)))
    mod = _modulation(c_pad, w_mod, b_mod).reshape(depth * mod_rows * N_MOD, 1, d)

    xt = x.reshape(batch * seq, d)
    for l in range(depth):
        xt, ffn_w = _ffn(xt, mod, norm_pre_t, norm_post_t, ffn_w, (ffn_f32, l, 1), l, 0, seq, mod_rows)
        zx, qkv, gt = _in_proj(xt, mod, norm_pre_t, w_in_groups, l, seq, mod_rows)
        dt_sp, cum = _gates(gt, gate_bias, l, batch, seq)
        y_ssm = _ssd(zx, dt_sp, conv_w, conv_b3, a_log_pad, d_skip_x, ssm_norm3, l, batch, seq)
        y_attn = _attention(qkv, cum, attn_norm3, l, batch, seq, q_blk, k_blk, v_blk)
        xt = _out_proj(xt, y_ssm, y_attn, wo, mod, norm_post_t, l, seq, mod_rows)
        following = (ffn_f32, l + 1, 0) if l + 1 < depth else None
        xt, ffn_w = _ffn(xt, mod, norm_pre_t, norm_post_t, ffn_w, following, l, 2, seq, mod_rows)
    return xt.reshape(batch, seq, d)
```

```python
import functools

import jax
import jax.numpy as jnp
from jax import lax
from jax.experimental import pallas as pl
from jax.experimental.pallas import tpu as pltpu

F32 = jnp.float32
BF16 = jnp.bfloat16

EPS = 1e-6
N_MOD = 9
SSM_HEAD_DIM = 64
SSM_HEADS = 16
SSM_GROUPS = 2
SSM_STATE = 128
CONV_WIDTH = 4
ATTN_HEAD_DIM = 128
ATTN_HEADS = 8
LANES = 128
SUBLANES = 8
VMEM_LIMIT_BYTES = 56 * 1024 * 1024

FFN_TM, FFN_TF, FFN_TN = 512, 512, 512
ROW_CHUNK, ROW_UNROLL = 16, 4
PROJ_TM = 512
OUT_TM = 512
MOD_TN = 1024
SPLIT_COLS = 256
SSD_CHUNK = 128
ATTN_TQ = 512
LOG2E = 1.4426950408889634
GATE_BLK = 128


def _params(*sem):
    return pltpu.CompilerParams(dimension_semantics=sem, vmem_limit_bytes=VMEM_LIMIT_BYTES)


def _silu(x):
    return x / (1.0 + jnp.exp(-x))


def _rms(x, g):
    return x * lax.rsqrt(jnp.mean(x * x, axis=-1, keepdims=True) + EPS) * g


def _norm_mod(x, g, shift, scale):
    return _rms(x, g) * (1.0 + scale) + shift


def _split3(x):
    hi = x.astype(BF16)
    r = x - hi.astype(F32)
    mid = r.astype(BF16)
    lo = (r - mid.astype(F32)).astype(BF16)
    return hi, mid, lo


def _dot(a, b):
    return jnp.dot(a, b, preferred_element_type=F32)


def _dot3(m01, x):
    hi, mid, lo = _split3(x)
    return _dot(m01, hi) + _dot(m01, mid) + _dot(m01, lo)


def _dot3r(x, m01):
    hi, mid, lo = _split3(x)
    return _dot(hi, m01) + _dot(mid, m01) + _dot(lo, m01)


def _tril_ones(n):
    r = lax.broadcasted_iota(jnp.int32, (n, n), 0)
    c = lax.broadcasted_iota(jnp.int32, (n, n), 1)
    return r >= c


def _mod_kernel(c_ref, w_ref, b_ref, o_ref):
    ca = _silu(c_ref[...]).astype(BF16)
    o_ref[...] = _dot(ca, w_ref[...].astype(BF16)) + b_ref[...]


def _modulation(c_pad, w_mod, b_mod):
    depth, d, n = w_mod.shape
    rows = c_pad.shape[0]
    return pl.pallas_call(
        _mod_kernel,
        grid=(depth, n // MOD_TN),
        in_specs=[
            pl.BlockSpec((rows, d), lambda l, j: (0, 0)),
            pl.BlockSpec((None, d, MOD_TN), lambda l, j: (l, 0, j)),
            pl.BlockSpec((None, 1, MOD_TN), lambda l, j: (l, 0, j)),
        ],
        out_specs=pl.BlockSpec((None, rows, MOD_TN), lambda l, j: (l, 0, j)),
        out_shape=jax.ShapeDtypeStruct((depth, rows, n), F32),
        compiler_params=_params("arbitrary", "arbitrary"),
        name="modulation",
    )(c_pad, w_mod, b_mod.reshape(depth, 1, n))


def _norm_mod_rows(x_ref, g_ref, sh_ref, sc_ref, gs_ref, h_ref):
    gs_ref[...] = g_ref[...] * (1.0 + sc_ref[...])

    def body(r, carry):
        rows = pl.ds(pl.multiple_of(r * ROW_CHUNK, ROW_CHUNK), ROW_CHUNK)
        xv = x_ref[rows, :]
        rs = lax.rsqrt(jnp.mean(xv * xv, axis=-1, keepdims=True) + EPS)
        h_ref[rows, :] = (xv * rs * gs_ref[...] + sh_ref[...]).astype(BF16)
        return carry

    lax.fori_loop(0, x_ref.shape[0] // ROW_CHUNK, body, 0, unroll=ROW_UNROLL)


def _ffn_kernel(x_ref, gpre_ref, sh_ref, sc_ref, gate_ref, gpost_ref, wg_ref, wu_ref, wd_ref, *rest,
                cast_next):
    if cast_next:
        nwg_ref, nwu_ref, nwd_ref, o_ref, cwg_ref, cwu_ref, cwd_ref, h_ref, a_ref, y_ref, gs_ref = rest
    else:
        o_ref, h_ref, a_ref, y_ref, gs_ref = rest
    s = pl.program_id(1)
    nf, tm, _ = a_ref.shape
    nd, _, tn = y_ref.shape
    d = x_ref.shape[1]

    @pl.when(s == 0)
    def _():
        _norm_mod_rows(x_ref, gpre_ref, sh_ref, sc_ref, gs_ref, h_ref)

    @pl.when(s < nf)
    def _():
        h = h_ref[...]
        g = _dot(h, wg_ref[...])
        u = _dot(h, wu_ref[...])
        a_ref[s] = (_silu(g) * u).astype(BF16)
        if cast_next:
            cwd_ref[...] = nwd_ref[...].astype(BF16)

    @pl.when(s >= nf)
    def _():
        a = jnp.concatenate([a_ref[k] for k in range(nf)], axis=1)
        y_ref[s - nf] = _dot(a, wd_ref[...])
        if cast_next:
            cwg_ref[...] = nwg_ref[...].astype(BF16)
            cwu_ref[...] = nwu_ref[...].astype(BF16)

    @pl.when(s == nf + nd - 1)
    def _():
        gs_ref[...] = (0.5 * gate_ref[...]) * gpost_ref[...]

        def body(r, carry):
            rows = pl.ds(pl.multiple_of(r * ROW_CHUNK, ROW_CHUNK), ROW_CHUNK)
            ys = [y_ref[n, rows, :] for n in range(nd)]
            ss = sum(jnp.sum(y * y, axis=-1, keepdims=True) for y in ys)
            rs = lax.rsqrt(ss * (1.0 / d) + EPS)
            for n in range(nd):
                cols = slice(n * tn, (n + 1) * tn)
                o_ref[rows, cols] = x_ref[rows, cols] + ys[n] * rs * gs_ref[:, cols]
            return carry

        lax.fori_loop(0, tm // ROW_CHUNK, body, 0, unroll=ROW_UNROLL)


def _vec_spec(row_fn):
    return lambda d: pl.BlockSpec((None, 1, d), lambda i, j: (row_fn(i), 0, 0))


def _ffn(x, mod, norm_pre, norm_post, weights, next_weights, layer, sub, seq, mod_rows):
    wg, wu, wd = weights
    t, d = x.shape
    dff = wg.shape[-1]
    nf, nd = dff // FFN_TF, d // FFN_TN
    nblk = t // FFN_TM
    per_b = seq // FFN_TM
    mrow = lambda j: (lambda i: (layer * mod_rows + i // per_b) * N_MOD + j)
    nrow = lambda i: layer * 3 + sub
    spec = lambda fn: _vec_spec(fn)(d)
    up_tile = lambda i, s: (0, jnp.minimum(s, nf - 1))
    in_specs = [
        pl.BlockSpec((FFN_TM, d), lambda i, s: (i, 0)),
        spec(nrow), spec(mrow(3 * sub)), spec(mrow(3 * sub + 1)), spec(mrow(3 * sub + 2)),
        spec(nrow),
        pl.BlockSpec((d, FFN_TF), up_tile),
        pl.BlockSpec((d, FFN_TF), up_tile),
        pl.BlockSpec((dff, FFN_TN), lambda i, s: (0, jnp.maximum(s - nf, 0))),
    ]
    out_specs = [pl.BlockSpec((FFN_TM, d), lambda i, s: (i, 0))]
    out_shape = [jax.ShapeDtypeStruct((t, d), F32)]
    operands = [x, norm_pre, mod, mod, mod, norm_post, wg, wu, wd]
    if next_weights is not None:
        (nwg, nwu, nwd), nl, ns = next_weights
        up_rows, dn_rows = d // (nblk * nd), dff // (nblk * nf)
        assert up_rows % (2 * SUBLANES) == 0 and dn_rows % (2 * SUBLANES) == 0
        up_slab = lambda i, s: i * nd + jnp.clip(s - nf, 0, nd - 1)
        dn_slab = lambda i, s: i * nf + jnp.minimum(s, nf - 1)
        in_specs += [
            pl.BlockSpec((None, None, up_rows, dff), lambda i, s: (nl, ns, up_slab(i, s), 0)),
            pl.BlockSpec((None, None, up_rows, dff), lambda i, s: (nl, ns, up_slab(i, s), 0)),
            pl.BlockSpec((None, None, dn_rows, d), lambda i, s: (nl, ns, dn_slab(i, s), 0)),
        ]
        out_specs += [
            pl.BlockSpec((up_rows, dff), lambda i, s: (up_slab(i, s), 0)),
            pl.BlockSpec((up_rows, dff), lambda i, s: (up_slab(i, s), 0)),
            pl.BlockSpec((dn_rows, d), lambda i, s: (dn_slab(i, s), 0)),
        ]
        out_shape += [jax.ShapeDtypeStruct((d, dff), BF16), jax.ShapeDtypeStruct((d, dff), BF16),
                      jax.ShapeDtypeStruct((dff, d), BF16)]
        operands += [nwg, nwu, nwd]
    outs = pl.pallas_call(
        functools.partial(_ffn_kernel, cast_next=next_weights is not None),
        grid=(nblk, nf + nd),
        in_specs=in_specs,
        out_specs=out_specs,
        out_shape=out_shape,
        scratch_shapes=[
            pltpu.VMEM((FFN_TM, d), BF16),
            pltpu.VMEM((nf, FFN_TM, FFN_TF), BF16),
            pltpu.VMEM((nd, FFN_TM, FFN_TN), F32),
            pltpu.VMEM((1, d), F32),
        ],
        compiler_params=_params("arbitrary", "arbitrary"),
        name="swiglu_half_step",
    )(*operands)
    return outs[0], tuple(outs[1:])


def _split_w_in(w_in_t, groups):
    depth, n, d = w_in_t.shape
    widths = [-(-sum(hi - lo for lo, hi in ranges) // LANES) * LANES for ranges in groups]

    def body(w_ref, *o_refs):
        for ranges, width, o_ref in zip(groups, widths, o_refs):
            pieces = [w_ref[lo:hi, :] for lo, hi in ranges]
            have = sum(hi - lo for lo, hi in ranges)
            if have < width:
                pieces.append(jnp.zeros((width - have, SPLIT_COLS), F32))
            rows = pieces[0] if len(pieces) == 1 else jnp.concatenate(pieces, axis=0)
            o_ref[...] = rows.T.astype(BF16)

    return pl.pallas_call(
        body,
        grid=(depth, d // SPLIT_COLS),
        in_specs=[pl.BlockSpec((None, n, SPLIT_COLS), lambda l, c: (l, 0, c))],
        out_specs=[pl.BlockSpec((None, SPLIT_COLS, w), lambda l, c: (l, c, 0)) for w in widths],
        out_shape=[jax.ShapeDtypeStruct((depth, d, w), BF16) for w in widths],
        compiler_params=_params("arbitrary", "arbitrary"),
        name="split_w_in",
    )(w_in_t)


def _in_proj(x, mod, norm_pre, weights, out_dtypes, layer, seq, mod_rows):
    t, d = x.shape
    per_b = seq // PROJ_TM
    row1 = lambda fn: pl.BlockSpec((None, 1, d), lambda i: (fn(i), 0, 0))
    mrow = lambda j: (lambda i: (layer * mod_rows + i // per_b) * N_MOD + j)
    nw = len(weights)

    def body(x_ref, gpre_ref, sh_ref, sc_ref, *rest):
        w_refs, o_refs, (h_ref, gs_ref) = rest[:nw], rest[nw:2 * nw], rest[2 * nw:]
        _norm_mod_rows(x_ref, gpre_ref, sh_ref, sc_ref, gs_ref, h_ref)
        h = h_ref[...]
        for w_ref, o_ref in zip(w_refs, o_refs):
            o_ref[...] = _dot(h, w_ref[...]).astype(o_ref.dtype)

    return pl.pallas_call(
        body,
        grid=(t // PROJ_TM,),
        in_specs=[
            pl.BlockSpec((PROJ_TM, d), lambda i: (i, 0)),
            row1(lambda i: layer * 3 + 1), row1(mrow(3)), row1(mrow(4)),
        ] + [pl.BlockSpec((None,) + w.shape[1:], lambda i: (layer, 0, 0), pipeline_mode=pl.Buffered(1))
             for w in weights],
        out_specs=[pl.BlockSpec((PROJ_TM, w.shape[-1]), lambda i: (i, 0)) for w in weights],
        out_shape=[jax.ShapeDtypeStruct((t, w.shape[-1]), dt) for w, dt in zip(weights, out_dtypes)],
        scratch_shapes=[pltpu.VMEM((PROJ_TM, d), BF16), pltpu.VMEM((1, d), F32)],
        compiler_params=_params("arbitrary"),
        name="mixer_in_proj",
    )(x, norm_pre, mod, mod, *weights)


def _gates_kernel(p_ref, bias_ref, dt_ref, cum_ref):
    seq = p_ref.shape[0]
    tril = _tril_ones(GATE_BLK).astype(BF16)
    carry = jnp.zeros((1, LANES), F32)
    for blk in range(seq // GATE_BLK):
        rows = pl.ds(blk * GATE_BLK, GATE_BLK)
        u = p_ref[rows, :] + bias_ref[...]
        t = jnp.log1p(jnp.exp(-jnp.abs(u)))
        dt_ref[rows, :] = jnp.maximum(u, 0.0) + t
        log_f = jnp.minimum(u, 0.0) - t
        cs = _dot3(tril, log_f) + carry
        cum_ref[rows, :] = cs
        carry = cs[GATE_BLK - 1:GATE_BLK, :]


def _gates(proj, gate_bias, layer, batch, seq):
    t = proj.shape[0]
    return pl.pallas_call(
        _gates_kernel,
        grid=(batch,),
        in_specs=[
            pl.BlockSpec((seq, LANES), lambda b: (b, 0)),
            pl.BlockSpec((None, 1, LANES), lambda b: (layer, 0, 0)),
        ],
        out_specs=[
            pl.BlockSpec((seq, LANES), lambda b: (b, 0)),
            pl.BlockSpec((seq, LANES), lambda b: (b, 0)),
        ],
        out_shape=[
            jax.ShapeDtypeStruct((t, LANES), F32),
            jax.ShapeDtypeStruct((t, LANES), F32),
        ],
        compiler_params=_params("arbitrary"),
        name="mixer_gates",
    )(proj, gate_bias)


def _ssd_kernel(xs_ref, bc_ref, z_ref, dt_ref, cw_ref, cb_ref, alog_ref, dskip_ref, gn_ref,
                o_ref, ubuf_ref, state_ref, ydiag_ref):
    lc = xs_ref.shape[0]
    d_ssm = xs_ref.shape[1]
    gw = d_ssm // SSM_GROUPS
    hpg = SSM_HEADS // SSM_GROUPS

    @pl.when(pl.program_id(1) == 0)
    def _():
        ubuf_ref[0:SUBLANES, :] = jnp.zeros((SUBLANES, ubuf_ref.shape[1]), F32)
        state_ref[...] = jnp.zeros(state_ref.shape, F32)

    ubuf_ref[SUBLANES:, 0:d_ssm] = xs_ref[...]
    ubuf_ref[SUBLANES:, d_ssm:] = bc_ref[...]
    u_ext = ubuf_ref[...]
    acc = cw_ref[0:1, :] * u_ext
    for k in range(1, CONV_WIDTH):
        acc = pltpu.roll(acc, 1, axis=0) + cw_ref[k:k + 1, :] * u_ext
    ubuf_ref[0:SUBLANES, :] = ubuf_ref[lc:lc + SUBLANES, :]
    xbc = _silu(acc[SUBLANES:, :] + cb_ref[...])
    xs = xbc[:, 0:d_ssm]
    b_mat = xbc[:, d_ssm:d_ssm + SSM_GROUPS * SSM_STATE].astype(BF16)
    c_mat = xbc[:, d_ssm + SSM_GROUPS * SSM_STATE:].astype(BF16)

    lane = lax.broadcasted_iota(jnp.int32, (1, LANES), 1)
    a = jnp.where(lane < SSM_HEADS, -jnp.exp(alog_ref[...]), 0.0)
    dt = dt_ref[...]
    tri = _tril_ones(lc)
    a_cs = _dot3(tri.astype(BF16), dt * a)
    a_cs_t = a_cs.T
    a_last = a_cs[lc - 1:lc, :]

    er = lax.broadcasted_iota(jnp.int32, (LANES, d_ssm), 0)
    ec = lax.broadcasted_iota(jnp.int32, (LANES, d_ssm), 1)
    expand = (ec // SSM_HEAD_DIM == er).astype(BF16)
    dt_x = _dot3r(dt, expand)
    dec_in = _dot3r(jnp.exp(a_cs), expand)
    dec_out = _dot3r(jnp.exp(a_last - a_cs), expand)
    xdt = xs * dt_x

    for g in range(SSM_GROUPS):
        cg = c_mat[:, g * SSM_STATE:(g + 1) * SSM_STATE]
        bg = b_mat[:, g * SSM_STATE:(g + 1) * SSM_STATE]
        cb = lax.dot_general(cg, bg, (((1,), (1,)), ((), ())), preferred_element_type=F32)
        for hh in range(hpg):
            h = g * hpg + hh
            seg = a_cs[:, h:h + 1] - a_cs_t[h:h + 1, :]
            m = (cb * jnp.exp(jnp.where(tri, seg, -jnp.inf))).astype(BF16)
            cols = slice(h * SSM_HEAD_DIM, (h + 1) * SSM_HEAD_DIM)
            ydiag_ref[:, cols] = _dot(m, xdt[:, cols].astype(BF16))

    xw = (xdt * dec_out).astype(BF16)
    y_parts = []
    for g in range(SSM_GROUPS):
        cols = slice(g * gw, (g + 1) * gw)
        cg = c_mat[:, g * SSM_STATE:(g + 1) * SSM_STATE]
        bg = b_mat[:, g * SSM_STATE:(g + 1) * SSM_STATE]
        st = state_ref[g]
        y_parts.append(_dot(cg, st.astype(BF16)))
        upd = lax.dot_general(bg, xw[:, cols], (((0,), (0,)), ((), ())), preferred_element_type=F32)
        state_ref[g] = st * dec_in[lc - 1:lc, cols] + upd
    y_off = jnp.concatenate(y_parts, axis=1) * dec_in

    y = (ydiag_ref[...] + y_off + dskip_ref[...] * xs) * _silu(z_ref[...])
    outs = []
    for g in range(SSM_GROUPS):
        cols = slice(g * gw, (g + 1) * gw)
        outs.append(_rms(y[:, cols], gn_ref[:, cols]))
    o_ref[...] = jnp.concatenate(outs, axis=1).astype(o_ref.dtype)


def _ssd(proj, dt_sp, conv_w, conv_b, a_log_pad, d_skip_x, ssm_norm, layer, batch, seq):
    t = proj.shape[0]
    d_ssm = SSM_HEADS * SSM_HEAD_DIM
    d_bc = 2 * SSM_GROUPS * SSM_STATE
    conv_dim = d_ssm + d_bc
    nc = seq // SSD_CHUNK
    row = lambda b, c: b * nc + c
    par = lambda width: pl.BlockSpec((None, 1, width), lambda b, c: (layer, 0, 0))
    return pl.pallas_call(
        _ssd_kernel,
        grid=(batch, nc),
        in_specs=[
            pl.BlockSpec((SSD_CHUNK, d_ssm), lambda b, c: (row(b, c), 1)),
            pl.BlockSpec((SSD_CHUNK, d_bc), lambda b, c: (row(b, c), 2 * d_ssm // d_bc)),
            pl.BlockSpec((SSD_CHUNK, d_ssm), lambda b, c: (row(b, c), 0)),
            pl.BlockSpec((SSD_CHUNK, LANES), lambda b, c: (row(b, c), 0)),
            pl.BlockSpec((None, CONV_WIDTH, conv_dim), lambda b, c: (layer, 0, 0)),
            par(conv_dim), par(LANES), par(d_ssm), par(d_ssm),
        ],
        out_specs=pl.BlockSpec((SSD_CHUNK, d_ssm), lambda b, c: (row(b, c), 0)),
        out_shape=jax.ShapeDtypeStruct((t, d_ssm), BF16),
        scratch_shapes=[
            pltpu.VMEM((SSD_CHUNK + SUBLANES, conv_dim), F32),
            pltpu.VMEM((SSM_GROUPS, SSM_STATE, d_ssm // SSM_GROUPS), F32),
            pltpu.VMEM((SSD_CHUNK, d_ssm), F32),
        ],
        compiler_params=_params("arbitrary", "arbitrary"),
        name="ssd_heads",
    )(proj, proj, proj, dt_sp, conv_w, conv_b, a_log_pad, d_skip_x, ssm_norm)


def _attn_kernel(q_ref, k_ref, v_ref, cum_ref, gn_ref, o_ref, qa_ref, ka_ref, vt_ref, p_ref):
    h = pl.program_id(1)
    seq = q_ref.shape[0]
    tq = p_ref.shape[1]
    lane = lax.broadcasted_iota(jnp.int32, (seq, LANES), 1)
    c2 = LOG2E * jnp.sum(jnp.where(lane == 2 * SUBLANES + h, cum_ref[...], 0.0), axis=1, keepdims=True)
    hi, mid, lo = (piece.astype(F32) for piece in _split3(c2))
    aug_q = jnp.where(lane == 0, hi, jnp.where(lane == 1, mid, jnp.where(lane == 2, lo,
                      jnp.where(lane < 6, 1.0, 0.0))))
    aug_k = jnp.where(lane < 3, 1.0, jnp.where(lane == 3, -hi, jnp.where(lane == 4, -mid,
                      jnp.where(lane == 5, -lo, 0.0))))
    qa_ref[:, 0:LANES] = (q_ref[...].astype(F32) * (LOG2E * ATTN_HEAD_DIM ** -0.5)).astype(BF16)
    qa_ref[:, LANES:] = aug_q.astype(BF16)
    ka_ref[:, 0:LANES] = k_ref[...].astype(BF16)
    ka_ref[:, LANES:] = aug_k.astype(BF16)
    vt_ref[...] = v_ref[...].astype(F32).T.astype(BF16)

    srow = lax.broadcasted_iota(jnp.int32, (tq, tq), 0)
    tcol = lax.broadcasted_iota(jnp.int32, (tq, tq), 1)
    causal = srow <= tcol
    nt = (((1,), (1,)), ((), ()))
    for i in range(seq // tq):
        lo_k, hi_k = i * tq, (i + 1) * tq
        qa = qa_ref[lo_k:hi_k, :]
        s_diag = jnp.where(causal, lax.dot_general(ka_ref[lo_k:hi_k, :], qa, nt, preferred_element_type=F32),
                           -jnp.inf)
        m = jnp.max(s_diag, axis=0, keepdims=True)
        if i > 0:
            s_past = lax.dot_general(ka_ref[0:lo_k, :], qa, nt, preferred_element_type=F32)
            m = jnp.maximum(m, jnp.max(s_past, axis=0, keepdims=True))
            p_past = jnp.exp2(s_past - m)
            p_ref[0:lo_k, :] = p_past.astype(BF16)
        p_diag = jnp.exp2(s_diag - m)
        p_ref[lo_k:hi_k, :] = p_diag.astype(BF16)
        l = jnp.sum(p_diag, axis=0, keepdims=True)
        if i > 0:
            l = l + jnp.sum(p_past, axis=0, keepdims=True)
        o_t = _dot(vt_ref[:, 0:hi_k], p_ref[0:hi_k, :]) / l
        o_t = o_t * lax.rsqrt(jnp.mean(o_t * o_t, axis=0, keepdims=True) + EPS)
        o_ref[lo_k:hi_k, :] = (o_t.T * gn_ref[...]).astype(o_ref.dtype)


def _attention(proj, cum, attn_norm, layer, batch, seq, q_blk, k_blk, v_blk):
    t = proj.shape[0]
    d_attn = ATTN_HEADS * ATTN_HEAD_DIM
    col = lambda blk: pl.BlockSpec((seq, ATTN_HEAD_DIM), lambda b, h: (b, blk + h))
    return pl.pallas_call(
        _attn_kernel,
        grid=(batch, ATTN_HEADS),
        in_specs=[
            col(q_blk), col(k_blk), col(v_blk),
            pl.BlockSpec((seq, LANES), lambda b, h: (b, 0)),
            pl.BlockSpec((None, 1, ATTN_HEAD_DIM), lambda b, h: (layer, 0, h)),
        ],
        out_specs=pl.BlockSpec((seq, ATTN_HEAD_DIM), lambda b, h: (b, h)),
        out_shape=jax.ShapeDtypeStruct((t, d_attn), BF16),
        scratch_shapes=[
            pltpu.VMEM((seq, 2 * LANES), BF16),
            pltpu.VMEM((seq, 2 * LANES), BF16),
            pltpu.VMEM((ATTN_HEAD_DIM, seq), BF16),
            pltpu.VMEM((seq, ATTN_TQ), BF16),
        ],
        compiler_params=_params("arbitrary", "arbitrary"),
        name="forgetting_attention",
    )(proj, proj, proj, cum, attn_norm)


def _out_kernel(x_ref, ys_ref, ya_ref, w_ref, gate_ref, gpost_ref, o_ref):
    d_ssm = ys_ref.shape[1]
    h = _dot(ys_ref[...], w_ref[0:d_ssm, :]) + _dot(ya_ref[...], w_ref[d_ssm:, :])
    o_ref[...] = x_ref[...] + gate_ref[...] * _rms(h, gpost_ref[...])


def _out_proj(x, y_ssm, y_attn, w_out, mod, norm_post, layer, seq, mod_rows):
    t, d = x.shape
    per_b = seq // OUT_TM
    spec1 = lambda fn: pl.BlockSpec((None, 1, d), lambda i: (fn(i), 0, 0))
    return pl.pallas_call(
        _out_kernel,
        grid=(t // OUT_TM,),
        in_specs=[
            pl.BlockSpec((OUT_TM, d), lambda i: (i, 0)),
            pl.BlockSpec((OUT_TM, y_ssm.shape[1]), lambda i: (i, 0)),
            pl.BlockSpec((OUT_TM, y_attn.shape[1]), lambda i: (i, 0)),
            pl.BlockSpec((None,) + w_out.shape[1:], lambda i: (layer, 0, 0)),
            spec1(lambda i: (layer * mod_rows + i // per_b) * N_MOD + 5),
            spec1(lambda i: layer * 3 + 1),
        ],
        out_specs=pl.BlockSpec((OUT_TM, d), lambda i: (i, 0)),
        out_shape=jax.ShapeDtypeStruct((t, d), F32),
        compiler_params=_params("arbitrary"),
        name="mixer_out_proj",
    )(x, y_ssm, y_attn, w_out, mod, norm_post)


def kernel(x, c, norm_pre, norm_post, w_mod, b_mod, w_ffn_gate, w_ffn_up, w_ffn_down, w_in, conv_w,
           conv_b, dt_bias, a_log, d_skip, ssm_norm, f_bias, attn_norm, w_out):
    batch, seq, d = x.shape
    depth = w_mod.shape[0]
    d_ssm = SSM_HEADS * SSM_HEAD_DIM
    d_attn = ATTN_HEADS * ATTN_HEAD_DIM
    conv_dim = d_ssm + 2 * SSM_GROUPS * SSM_STATE
    assert seq % max(FFN_TM, PROJ_TM, OUT_TM, ATTN_TQ, SSD_CHUNK) == 0
    assert w_in.shape[-1] == 2 * d_ssm + 2 * SSM_GROUPS * SSM_STATE + SSM_HEADS + 3 * d_attn + ATTN_HEADS

    ffn_f32 = (w_ffn_gate, w_ffn_up, w_ffn_down)
    ffn_w = tuple(w[0, 0].astype(BF16) for w in ffn_f32)
    wo = w_out.astype(BF16)
    o_dt = d_ssm + conv_dim
    o_q = o_dt + SSM_HEADS
    o_f = o_q + 3 * d_attn
    pad = LANES - SSM_HEADS - ATTN_HEADS
    w_in_groups = _split_w_in(jnp.swapaxes(w_in, 1, 2),
                              (((0, o_dt),), ((o_q, o_f),), ((o_dt, o_q), (o_f, w_in.shape[-1]))))
    q_blk, k_blk, v_blk = 0, d_attn // LANES, 2 * d_attn // LANES

    gate_bias = jnp.concatenate([dt_bias, f_bias, jnp.zeros((depth, pad), F32)], axis=-1)[:, None, :]
    a_log_pad = jnp.pad(a_log, ((0, 0), (0, LANES - SSM_HEADS)))[:, None, :]
    d_skip_x = jnp.repeat(d_skip, SSM_HEAD_DIM, axis=-1)[:, None, :]
    conv_b3 = conv_b[:, None, :]
    ssm_norm3 = ssm_norm[:, None, :]
    attn_norm3 = attn_norm[:, None, :]
    norm_pre_t = norm_pre.reshape(depth * 3, 1, d)
    norm_post_t = norm_post.reshape(depth * 3, 1, d)

    mod_rows = 2 * SUBLANES
    c_pad = jnp.pad(c, ((0, mod_rows - batch), (0, 0)))
    mod = _modulation(c_pad, w_mod, b_mod).reshape(depth * mod_rows * N_MOD, 1, d)

    xt = x.reshape(batch * seq, d)
    for l in range(depth):
        xt, ffn_w = _ffn(xt, mod, norm_pre_t, norm_post_t, ffn_w, (ffn_f32, l, 1), l, 0, seq, mod_rows)
        zx, qkv, gt = _in_proj(xt, mod, norm_pre_t, w_in_groups, (F32, BF16, F32), l, seq, mod_rows)
        dt_sp, cum = _gates(gt, gate_bias, l, batch, seq)
        y_ssm = _ssd(zx, dt_sp, conv_w, conv_b3, a_log_pad, d_skip_x, ssm_norm3, l, batch, seq)
        y_attn = _attention(qkv, cum, attn_norm3, l, batch, seq, q_blk, k_blk, v_blk)
        xt = _out_proj(xt, y_ssm, y_attn, wo, mod, norm_post_t, l, seq, mod_rows)
        following = (ffn_f32, l + 1, 0) if l + 1 < depth else None
        xt, ffn_w = _ffn(xt, mod, norm_pre_t, norm_post_t, ffn_w, following, l, 2, seq, mod_rows)
    return xt.reshape(batch, seq, d)
```

```python
import functools

import jax
import jax.numpy as jnp
from jax import lax
from jax.experimental import pallas as pl
from jax.experimental.pallas import tpu as pltpu

F32 = jnp.float32
BF16 = jnp.bfloat16

EPS = 1e-6
N_MOD = 9
SSM_HEAD_DIM = 64
SSM_HEADS = 16
SSM_GROUPS = 2
SSM_STATE = 128
CONV_WIDTH = 4
ATTN_HEAD_DIM = 128
ATTN_HEADS = 8
LANES = 128
SUBLANES = 8
VMEM_LIMIT_BYTES = 56 * 1024 * 1024

FFN_TM, FFN_TF, FFN_TN = 512, 512, 512
ROW_CHUNK, ROW_UNROLL = 16, 4
PROJ_TM = 512
OUT_TM = 512
MOD_TN = 1024
SPLIT_COLS = 256
SSD_CHUNK = 128
ATTN_TQ = 512
LOG2E = 1.4426950408889634
GATE_BLK = 128


def _params(*sem):
    return pltpu.CompilerParams(dimension_semantics=sem, vmem_limit_bytes=VMEM_LIMIT_BYTES)


def _silu(x):
    return x / (1.0 + jnp.exp(-x))


def _rms(x, g):
    return x * lax.rsqrt(jnp.mean(x * x, axis=-1, keepdims=True) + EPS) * g


def _norm_mod(x, g, shift, scale):
    return _rms(x, g) * (1.0 + scale) + shift


def _split3(x):
    hi = x.astype(BF16)
    r = x - hi.astype(F32)
    mid = r.astype(BF16)
    lo = (r - mid.astype(F32)).astype(BF16)
    return hi, mid, lo


def _dot(a, b):
    return jnp.dot(a, b, preferred_element_type=F32)


def _dot3(m01, x):
    hi, mid, lo = _split3(x)
    return _dot(m01, hi) + _dot(m01, mid) + _dot(m01, lo)


def _dot3r(x, m01):
    hi, mid, lo = _split3(x)
    return _dot(hi, m01) + _dot(mid, m01) + _dot(lo, m01)


def _tril_ones(n):
    r = lax.broadcasted_iota(jnp.int32, (n, n), 0)
    c = lax.broadcasted_iota(jnp.int32, (n, n), 1)
    return r >= c


def _mod_kernel(c_ref, w_ref, b_ref, o_ref):
    ca = _silu(c_ref[...]).astype(BF16)
    o_ref[...] = _dot(ca, w_ref[...].astype(BF16)) + b_ref[...]


def _modulation(c_pad, w_mod, b_mod):
    depth, d, n = w_mod.shape
    rows = c_pad.shape[0]
    per_piece = d // MOD_TN
    mod = pl.pallas_call(
        _mod_kernel,
        grid=(depth, n // MOD_TN),
        in_specs=[
            pl.BlockSpec((rows, d), lambda l, j: (0, 0)),
            pl.BlockSpec((None, d, MOD_TN), lambda l, j: (l, 0, j)),
            pl.BlockSpec((None, 1, MOD_TN), lambda l, j: (l, 0, j)),
        ],
        out_specs=pl.BlockSpec((None, None, rows, MOD_TN), lambda l, j: (l, j // per_piece, 0, j % per_piece)),
        out_shape=jax.ShapeDtypeStruct((depth, N_MOD, rows, d), F32),
        compiler_params=_params("arbitrary", "arbitrary"),
        name="modulation",
    )(c_pad, w_mod, b_mod.reshape(depth, 1, n))
    return mod.reshape(depth, N_MOD // 3, 3, rows, d)


def _mod_spec(layer, sub, rows, d, grid_rank):
    zeros = (0,) * 3
    index = (lambda i: (layer, sub) + zeros) if grid_rank == 1 else (lambda i, s: (layer, sub) + zeros)
    return pl.BlockSpec((None, None, 3, rows, d), index)


def _norm_spec(layer, sub, d, grid_rank):
    index = (lambda i: (layer, sub, 0, 0)) if grid_rank == 1 else (lambda i, s: (layer, sub, 0, 0))
    return pl.BlockSpec((None, None, 2, d), index)


def _norm_mod_rows(x_ref, norm_ref, mod_ref, b, gs_ref, h_ref):
    gs_ref[0:1, :] = norm_ref[0:1, :] * (1.0 + mod_ref[1, pl.ds(b, 1), :])
    gs_ref[1:2, :] = mod_ref[0, pl.ds(b, 1), :]

    def body(r, carry):
        rows = pl.ds(pl.multiple_of(r * ROW_CHUNK, ROW_CHUNK), ROW_CHUNK)
        xv = x_ref[rows, :]
        rs = lax.rsqrt(jnp.mean(xv * xv, axis=-1, keepdims=True) + EPS)
        h_ref[rows, :] = (xv * rs * gs_ref[0:1, :] + gs_ref[1:2, :]).astype(BF16)
        return carry

    lax.fori_loop(0, x_ref.shape[0] // ROW_CHUNK, body, 0, unroll=ROW_UNROLL)


def _ffn_kernel(x_ref, norm_ref, mod_ref, wg_ref, wu_ref, wd_ref, *rest, cast_next, blocks_per_row):
    if cast_next:
        nwg_ref, nwu_ref, nwd_ref, o_ref, cwg_ref, cwu_ref, cwd_ref, h_ref, a_ref, y_ref, gs_ref = rest
    else:
        o_ref, h_ref, a_ref, y_ref, gs_ref = rest
    s = pl.program_id(1)
    b = pl.program_id(0) // blocks_per_row
    nf, tm, _ = a_ref.shape
    nd, _, tn = y_ref.shape
    d = x_ref.shape[1]

    @pl.when(s == 0)
    def _():
        _norm_mod_rows(x_ref, norm_ref, mod_ref, b, gs_ref, h_ref)

    @pl.when(s < nf)
    def _():
        h = h_ref[...]
        g = _dot(h, wg_ref[...])
        u = _dot(h, wu_ref[...])
        a_ref[s] = (_silu(g) * u).astype(BF16)
        if cast_next:
            cwd_ref[...] = nwd_ref[...].astype(BF16)

    @pl.when(s >= nf)
    def _():
        a = jnp.concatenate([a_ref[k] for k in range(nf)], axis=1)
        y_ref[s - nf] = _dot(a, wd_ref[...])
        if cast_next:
            cwg_ref[...] = nwg_ref[...].astype(BF16)
            cwu_ref[...] = nwu_ref[...].astype(BF16)

    @pl.when(s == nf + nd - 1)
    def _():
        gs_ref[0:1, :] = (0.5 * mod_ref[2, pl.ds(b, 1), :]) * norm_ref[1:2, :]

        def body(r, carry):
            rows = pl.ds(pl.multiple_of(r * ROW_CHUNK, ROW_CHUNK), ROW_CHUNK)
            ys = [y_ref[n, rows, :] for n in range(nd)]
            ss = sum(jnp.sum(y * y, axis=-1, keepdims=True) for y in ys)
            rs = lax.rsqrt(ss * (1.0 / d) + EPS)
            for n in range(nd):
                cols = slice(n * tn, (n + 1) * tn)
                o_ref[rows, cols] = x_ref[rows, cols] + ys[n] * rs * gs_ref[0:1, cols]
            return carry

        lax.fori_loop(0, tm // ROW_CHUNK, body, 0, unroll=ROW_UNROLL)


def _ffn(x, mod, norms, weights, next_weights, layer, sub, seq):
    wg, wu, wd = weights
    t, d = x.shape
    dff = wg.shape[-1]
    nf, nd = dff // FFN_TF, d // FFN_TN
    nblk = t // FFN_TM
    up_tile = lambda i, s: (0, jnp.minimum(s, nf - 1))
    in_specs = [
        pl.BlockSpec((FFN_TM, d), lambda i, s: (i, 0)),
        _norm_spec(layer, sub, d, 2),
        _mod_spec(layer, sub, mod.shape[-2], d, 2),
        pl.BlockSpec((d, FFN_TF), up_tile),
        pl.BlockSpec((d, FFN_TF), up_tile),
        pl.BlockSpec((dff, FFN_TN), lambda i, s: (0, jnp.maximum(s - nf, 0))),
    ]
    out_specs = [pl.BlockSpec((FFN_TM, d), lambda i, s: (i, 0))]
    out_shape = [jax.ShapeDtypeStruct((t, d), F32)]
    operands = [x, norms, mod, wg, wu, wd]
    if next_weights is not None:
        (nwg, nwu, nwd), nl, ns = next_weights
        up_rows, dn_rows = d // (nblk * nd), dff // (nblk * nf)
        assert up_rows % (2 * SUBLANES) == 0 and dn_rows % (2 * SUBLANES) == 0
        up_slab = lambda i, s: i * nd + jnp.clip(s - nf, 0, nd - 1)
        dn_slab = lambda i, s: i * nf + jnp.minimum(s, nf - 1)
        in_specs += [
            pl.BlockSpec((None, None, up_rows, dff), lambda i, s: (nl, ns, up_slab(i, s), 0)),
            pl.BlockSpec((None, None, up_rows, dff), lambda i, s: (nl, ns, up_slab(i, s), 0)),
            pl.BlockSpec((None, None, dn_rows, d), lambda i, s: (nl, ns, dn_slab(i, s), 0)),
        ]
        out_specs += [
            pl.BlockSpec((up_rows, dff), lambda i, s: (up_slab(i, s), 0)),
            pl.BlockSpec((up_rows, dff), lambda i, s: (up_slab(i, s), 0)),
            pl.BlockSpec((dn_rows, d), lambda i, s: (dn_slab(i, s), 0)),
        ]
        out_shape += [jax.ShapeDtypeStruct((d, dff), BF16), jax.ShapeDtypeStruct((d, dff), BF16),
                      jax.ShapeDtypeStruct((dff, d), BF16)]
        operands += [nwg, nwu, nwd]
    outs = pl.pallas_call(
        functools.partial(_ffn_kernel, cast_next=next_weights is not None, blocks_per_row=seq // FFN_TM),
        grid=(nblk, nf + nd),
        in_specs=in_specs,
        out_specs=out_specs,
        out_shape=out_shape,
        scratch_shapes=[
            pltpu.VMEM((FFN_TM, d), BF16),
            pltpu.VMEM((nf, FFN_TM, FFN_TF), BF16),
            pltpu.VMEM((nd, FFN_TM, FFN_TN), F32),
            pltpu.VMEM((2, d), F32),
        ],
        compiler_params=_params("arbitrary", "arbitrary"),
        name="swiglu_half_step",
    )(*operands)
    return outs[0], tuple(outs[1:])


def _split_w_in(w_in_t, groups):
    depth, n, d = w_in_t.shape
    widths = [-(-sum(hi - lo for lo, hi in ranges) // LANES) * LANES for ranges in groups]

    def body(w_ref, *o_refs):
        for ranges, width, o_ref in zip(groups, widths, o_refs):
            pieces = [w_ref[lo:hi, :] for lo, hi in ranges]
            have = sum(hi - lo for lo, hi in ranges)
            if have < width:
                pieces.append(jnp.zeros((width - have, SPLIT_COLS), F32))
            rows = pieces[0] if len(pieces) == 1 else jnp.concatenate(pieces, axis=0)
            o_ref[...] = rows.T.astype(BF16)

    return pl.pallas_call(
        body,
        grid=(depth, d // SPLIT_COLS),
        in_specs=[pl.BlockSpec((None, n, SPLIT_COLS), lambda l, c: (l, 0, c))],
        out_specs=[pl.BlockSpec((None, SPLIT_COLS, w), lambda l, c: (l, c, 0)) for w in widths],
        out_shape=[jax.ShapeDtypeStruct((depth, d, w), BF16) for w in widths],
        compiler_params=_params("arbitrary", "arbitrary"),
        name="split_w_in",
    )(w_in_t)


def _in_proj(x, mod, norms, weights, out_dtypes, layer, seq):
    t, d = x.shape
    per_b = seq // PROJ_TM
    nw = len(weights)

    def body(x_ref, norm_ref, mod_ref, *rest):
        w_refs, o_refs, (h_ref, gs_ref) = rest[:nw], rest[nw:2 * nw], rest[2 * nw:]
        _norm_mod_rows(x_ref, norm_ref, mod_ref, pl.program_id(0) // per_b, gs_ref, h_ref)
        h = h_ref[...]
        for w_ref, o_ref in zip(w_refs, o_refs):
            o_ref[...] = _dot(h, w_ref[...]).astype(o_ref.dtype)

    return pl.pallas_call(
        body,
        grid=(t // PROJ_TM,),
        in_specs=[
            pl.BlockSpec((PROJ_TM, d), lambda i: (i, 0)),
            _norm_spec(layer, 1, d, 1),
            _mod_spec(layer, 1, mod.shape[-2], d, 1),
        ] + [pl.BlockSpec((None,) + w.shape[1:], lambda i: (layer, 0, 0), pipeline_mode=pl.Buffered(1))
             for w in weights],
        out_specs=[pl.BlockSpec((PROJ_TM, w.shape[-1]), lambda i: (i, 0)) for w in weights],
        out_shape=[jax.ShapeDtypeStruct((t, w.shape[-1]), dt) for w, dt in zip(weights, out_dtypes)],
        scratch_shapes=[pltpu.VMEM((PROJ_TM, d), BF16), pltpu.VMEM((2, d), F32)],
        compiler_params=_params("arbitrary"),
        name="mixer_in_proj",
    )(x, norms, mod, *weights)


def _gates_kernel(p_ref, bias_ref, dt_ref, cum_ref):
    seq = p_ref.shape[0]
    tril = _tril_ones(GATE_BLK).astype(BF16)
    carry = jnp.zeros((1, LANES), F32)
    for blk in range(seq // GATE_BLK):
        rows = pl.ds(blk * GATE_BLK, GATE_BLK)
        u = p_ref[rows, :] + bias_ref[...]
        t = jnp.log1p(jnp.exp(-jnp.abs(u)))
        dt_ref[rows, :] = jnp.maximum(u, 0.0) + t
        log_f = jnp.minimum(u, 0.0) - t
        cs = _dot3(tril, log_f) + carry
        cum_ref[rows, :] = cs
        carry = cs[GATE_BLK - 1:GATE_BLK, :]


def _gates(proj, gate_bias, layer, batch, seq):
    t = proj.shape[0]
    return pl.pallas_call(
        _gates_kernel,
        grid=(batch,),
        in_specs=[
            pl.BlockSpec((seq, LANES), lambda b: (b, 0)),
            pl.BlockSpec((None, 1, LANES), lambda b: (layer, 0, 0)),
        ],
        out_specs=[
            pl.BlockSpec((seq, LANES), lambda b: (b, 0)),
            pl.BlockSpec((seq, LANES), lambda b: (b, 0)),
        ],
        out_shape=[
            jax.ShapeDtypeStruct((t, LANES), F32),
            jax.ShapeDtypeStruct((t, LANES), F32),
        ],
        compiler_params=_params("arbitrary"),
        name="mixer_gates",
    )(proj, gate_bias)


def _ssd_kernel(xs_ref, bc_ref, z_ref, dt_ref, cw_ref, cb_ref, alog_ref, dskip_ref, gn_ref,
                o_ref, ubuf_ref, state_ref, ydiag_ref):
    lc = xs_ref.shape[0]
    d_ssm = xs_ref.shape[1]
    gw = d_ssm // SSM_GROUPS
    hpg = SSM_HEADS // SSM_GROUPS

    @pl.when(pl.program_id(1) == 0)
    def _():
        ubuf_ref[0:SUBLANES, :] = jnp.zeros((SUBLANES, ubuf_ref.shape[1]), F32)
        state_ref[...] = jnp.zeros(state_ref.shape, F32)

    ubuf_ref[SUBLANES:, 0:d_ssm] = xs_ref[...]
    ubuf_ref[SUBLANES:, d_ssm:] = bc_ref[...]
    u_ext = ubuf_ref[...]
    acc = cw_ref[0:1, :] * u_ext
    for k in range(1, CONV_WIDTH):
        acc = pltpu.roll(acc, 1, axis=0) + cw_ref[k:k + 1, :] * u_ext
    ubuf_ref[0:SUBLANES, :] = ubuf_ref[lc:lc + SUBLANES, :]
    xbc = _silu(acc[SUBLANES:, :] + cb_ref[...])
    xs = xbc[:, 0:d_ssm]
    b_mat = xbc[:, d_ssm:d_ssm + SSM_GROUPS * SSM_STATE].astype(BF16)
    c_mat = xbc[:, d_ssm + SSM_GROUPS * SSM_STATE:].astype(BF16)

    lane = lax.broadcasted_iota(jnp.int32, (1, LANES), 1)
    a = jnp.where(lane < SSM_HEADS, -jnp.exp(alog_ref[...]), 0.0)
    dt = dt_ref[...]
    tri = _tril_ones(lc)
    a_cs = _dot3(tri.astype(BF16), dt * a)
    a_cs_t = a_cs.T
    a_last = a_cs[lc - 1:lc, :]

    er = lax.broadcasted_iota(jnp.int32, (LANES, d_ssm), 0)
    ec = lax.broadcasted_iota(jnp.int32, (LANES, d_ssm), 1)
    expand = (ec // SSM_HEAD_DIM == er).astype(BF16)
    dt_x = _dot3r(dt, expand)
    dec_in = _dot3r(jnp.exp(a_cs), expand)
    dec_out = _dot3r(jnp.exp(a_last - a_cs), expand)
    xdt = xs * dt_x

    for g in range(SSM_GROUPS):
        cg = c_mat[:, g * SSM_STATE:(g + 1) * SSM_STATE]
        bg = b_mat[:, g * SSM_STATE:(g + 1) * SSM_STATE]
        cb = lax.dot_general(cg, bg, (((1,), (1,)), ((), ())), preferred_element_type=F32)
        for hh in range(hpg):
            h = g * hpg + hh
            seg = a_cs[:, h:h + 1] - a_cs_t[h:h + 1, :]
            m = (cb * jnp.exp(jnp.where(tri, seg, -jnp.inf))).astype(BF16)
            cols = slice(h * SSM_HEAD_DIM, (h + 1) * SSM_HEAD_DIM)
            ydiag_ref[:, cols] = _dot(m, xdt[:, cols].astype(BF16))

    xw = (xdt * dec_out).astype(BF16)
    y_parts = []
    for g in range(SSM_GROUPS):
        cols = slice(g * gw, (g + 1) * gw)
        cg = c_mat[:, g * SSM_STATE:(g + 1) * SSM_STATE]
        bg = b_mat[:, g * SSM_STATE:(g + 1) * SSM_STATE]
        st = state_ref[g]
        y_parts.append(_dot(cg, st.astype(BF16)))
        upd = lax.dot_general(bg, xw[:, cols], (((0,), (0,)), ((), ())), preferred_element_type=F32)
        state_ref[g] = st * dec_in[lc - 1:lc, cols] + upd
    y_off = jnp.concatenate(y_parts, axis=1) * dec_in

    y = (ydiag_ref[...] + y_off + dskip_ref[...] * xs) * _silu(z_ref[...])
    outs = []
    for g in range(SSM_GROUPS):
        cols = slice(g * gw, (g + 1) * gw)
        outs.append(_rms(y[:, cols], gn_ref[:, cols]))
    o_ref[...] = jnp.concatenate(outs, axis=1).astype(o_ref.dtype)


def _ssd(proj, dt_sp, conv_w, conv_b, a_log_pad, d_skip_x, ssm_norm, layer, batch, seq):
    t = proj.shape[0]
    d_ssm = SSM_HEADS * SSM_HEAD_DIM
    d_bc = 2 * SSM_GROUPS * SSM_STATE
    conv_dim = d_ssm + d_bc
    nc = seq // SSD_CHUNK
    row = lambda b, c: b * nc + c
    par = lambda width: pl.BlockSpec((None, 1, width), lambda b, c: (layer, 0, 0))
    return pl.pallas_call(
        _ssd_kernel,
        grid=(batch, nc),
        in_specs=[
            pl.BlockSpec((SSD_CHUNK, d_ssm), lambda b, c: (row(b, c), 1)),
            pl.BlockSpec((SSD_CHUNK, d_bc), lambda b, c: (row(b, c), 2 * d_ssm // d_bc)),
            pl.BlockSpec((SSD_CHUNK, d_ssm), lambda b, c: (row(b, c), 0)),
            pl.BlockSpec((SSD_CHUNK, LANES), lambda b, c: (row(b, c), 0)),
            pl.BlockSpec((None, CONV_WIDTH, conv_dim), lambda b, c: (layer, 0, 0)),
            par(conv_dim), par(LANES), par(d_ssm), par(d_ssm),
        ],
        out_specs=pl.BlockSpec((SSD_CHUNK, d_ssm), lambda b, c: (row(b, c), 0)),
        out_shape=jax.ShapeDtypeStruct((t, d_ssm), BF16),
        scratch_shapes=[
            pltpu.VMEM((SSD_CHUNK + SUBLANES, conv_dim), F32),
            pltpu.VMEM((SSM_GROUPS, SSM_STATE, d_ssm // SSM_GROUPS), F32),
            pltpu.VMEM((SSD_CHUNK, d_ssm), F32),
        ],
        compiler_params=_params("arbitrary", "arbitrary"),
        name="ssd_heads",
    )(proj, proj, proj, dt_sp, conv_w, conv_b, a_log_pad, d_skip_x, ssm_norm)


def _attn_kernel(q_ref, k_ref, v_ref, cum_ref, gn_ref, o_ref, qa_ref, ka_ref, vt_ref):
    h = pl.program_id(1)
    seq = q_ref.shape[0]
    tq = ATTN_TQ
    lane = lax.broadcasted_iota(jnp.int32, (1, LANES), 1)
    c2 = LOG2E * jnp.sum(jnp.where(lane == 2 * SUBLANES + h, cum_ref[...], 0.0), axis=1, keepdims=True)
    hi, mid, lo = (piece.astype(F32) for piece in _split3(c2))
    aug_q = jnp.where(lane == 0, hi, jnp.where(lane == 1, mid, jnp.where(lane == 2, lo,
                      jnp.where(lane < 6, 1.0, 0.0))))
    aug_k = jnp.where(lane < 3, 1.0, jnp.where(lane == 3, -hi, jnp.where(lane == 4, -mid,
                      jnp.where(lane == 5, -lo, 0.0))))
    qa_ref[:, 0:LANES] = (q_ref[...].astype(F32) * (LOG2E * ATTN_HEAD_DIM ** -0.5)).astype(BF16)
    qa_ref[:, LANES:] = aug_q.astype(BF16)
    ka_ref[:, 0:LANES] = k_ref[...].astype(BF16)
    ka_ref[:, LANES:] = aug_k.astype(BF16)
    vt_ref[...] = v_ref[...].astype(F32).T.astype(BF16)

    srow = lax.broadcasted_iota(jnp.int32, (tq, tq), 0)
    tcol = lax.broadcasted_iota(jnp.int32, (tq, tq), 1)
    causal = srow <= tcol
    nt = (((1,), (1,)), ((), ()))
    scores = []
    for i in range(seq // tq):
        lo_k, hi_k = i * tq, (i + 1) * tq
        qa = qa_ref[lo_k:hi_k, :]
        s_diag = jnp.where(causal, lax.dot_general(ka_ref[lo_k:hi_k, :], qa, nt, preferred_element_type=F32),
                           -jnp.inf)
        s_past = lax.dot_general(ka_ref[0:lo_k, :], qa, nt, preferred_element_type=F32) if i > 0 else None
        scores.append((s_diag, s_past))
    for i, (s_diag, s_past) in enumerate(scores):
        lo_k, hi_k = i * tq, (i + 1) * tq
        m = jnp.max(s_diag, axis=0, keepdims=True)
        if i > 0:
            m = jnp.maximum(m, jnp.max(s_past, axis=0, keepdims=True))
        p = jnp.exp2(s_diag - m)
        if i > 0:
            p = jnp.concatenate([jnp.exp2(s_past - m), p], axis=0)
        l = jnp.sum(p, axis=0, keepdims=True)
        o_t = _dot(vt_ref[:, 0:hi_k], p.astype(BF16)) / l
        o_t = o_t * lax.rsqrt(jnp.mean(o_t * o_t, axis=0, keepdims=True) + EPS)
        o_ref[lo_k:hi_k, :] = (o_t.T * gn_ref[...]).astype(o_ref.dtype)


def _attention(proj, cum, attn_norm, layer, batch, seq, q_blk, k_blk, v_blk):
    t = proj.shape[0]
    d_attn = ATTN_HEADS * ATTN_HEAD_DIM
    col = lambda blk: pl.BlockSpec((seq, ATTN_HEAD_DIM), lambda b, h: (b, blk + h))
    return pl.pallas_call(
        _attn_kernel,
        grid=(batch, ATTN_HEADS),
        in_specs=[
            col(q_blk), col(k_blk), col(v_blk),
            pl.BlockSpec((seq, LANES), lambda b, h: (b, 0)),
            pl.BlockSpec((None, 1, ATTN_HEAD_DIM), lambda b, h: (layer, 0, h)),
        ],
        out_specs=pl.BlockSpec((seq, ATTN_HEAD_DIM), lambda b, h: (b, h)),
        out_shape=jax.ShapeDtypeStruct((t, d_attn), BF16),
        scratch_shapes=[
            pltpu.VMEM((seq, 2 * LANES), BF16),
            pltpu.VMEM((seq, 2 * LANES), BF16),
            pltpu.VMEM((ATTN_HEAD_DIM, seq), BF16),
        ],
        compiler_params=_params("arbitrary", "arbitrary"),
        name="forgetting_attention",
    )(proj, proj, proj, cum, attn_norm)


def _out_kernel(x_ref, ys_ref, ya_ref, w_ref, norm_ref, mod_ref, o_ref, *, blocks_per_row):
    d_ssm = ys_ref.shape[1]
    gate = mod_ref[2, pl.ds(pl.program_id(0) // blocks_per_row, 1), :]
    h = _dot(ys_ref[...], w_ref[0:d_ssm, :]) + _dot(ya_ref[...], w_ref[d_ssm:, :])
    o_ref[...] = x_ref[...] + gate * _rms(h, norm_ref[1:2, :])


def _out_proj(x, y_ssm, y_attn, w_out, mod, norms, layer, seq):
    t, d = x.shape
    return pl.pallas_call(
        functools.partial(_out_kernel, blocks_per_row=seq // OUT_TM),
        grid=(t // OUT_TM,),
        in_specs=[
            pl.BlockSpec((OUT_TM, d), lambda i: (i, 0)),
            pl.BlockSpec((OUT_TM, y_ssm.shape[1]), lambda i: (i, 0)),
            pl.BlockSpec((OUT_TM, y_attn.shape[1]), lambda i: (i, 0)),
            pl.BlockSpec((None,) + w_out.shape[1:], lambda i: (layer, 0, 0)),
            _norm_spec(layer, 1, d, 1),
            _mod_spec(layer, 1, mod.shape[-2], d, 1),
        ],
        out_specs=pl.BlockSpec((OUT_TM, d), lambda i: (i, 0)),
        out_shape=jax.ShapeDtypeStruct((t, d), F32),
        compiler_params=_params("arbitrary"),
        name="mixer_out_proj",
    )(x, y_ssm, y_attn, w_out, norms, mod)


def kernel(x, c, norm_pre, norm_post, w_mod, b_mod, w_ffn_gate, w_ffn_up, w_ffn_down, w_in, conv_w,
           conv_b, dt_bias, a_log, d_skip, ssm_norm, f_bias, attn_norm, w_out):
    batch, seq, d = x.shape
    depth = w_mod.shape[0]
    d_ssm = SSM_HEADS * SSM_HEAD_DIM
    d_attn = ATTN_HEADS * ATTN_HEAD_DIM
    conv_dim = d_ssm + 2 * SSM_GROUPS * SSM_STATE
    assert seq % max(FFN_TM, PROJ_TM, OUT_TM, ATTN_TQ, SSD_CHUNK) == 0
    assert w_in.shape[-1] == 2 * d_ssm + 2 * SSM_GROUPS * SSM_STATE + SSM_HEADS + 3 * d_attn + ATTN_HEADS

    ffn_f32 = (w_ffn_gate, w_ffn_up, w_ffn_down)
    ffn_w = tuple(w[0, 0].astype(BF16) for w in ffn_f32)
    wo = w_out.astype(BF16)
    o_dt = d_ssm + conv_dim
    o_q = o_dt + SSM_HEADS
    o_f = o_q + 3 * d_attn
    pad = LANES - SSM_HEADS - ATTN_HEADS
    w_in_groups = _split_w_in(jnp.swapaxes(w_in, 1, 2),
                              (((0, o_dt),), ((o_q, o_f),), ((o_dt, o_q), (o_f, w_in.shape[-1]))))
    q_blk, k_blk, v_blk = 0, d_attn // LANES, 2 * d_attn // LANES

    gate_bias = jnp.concatenate([dt_bias, f_bias, jnp.zeros((depth, pad), F32)], axis=-1)[:, None, :]
    a_log_pad = jnp.pad(a_log, ((0, 0), (0, LANES - SSM_HEADS)))[:, None, :]
    d_skip_x = jnp.repeat(d_skip, SSM_HEAD_DIM, axis=-1)[:, None, :]
    conv_b3 = conv_b[:, None, :]
    ssm_norm3 = ssm_norm[:, None, :]
    attn_norm3 = attn_norm[:, None, :]
    norms = jnp.stack([norm_pre, norm_post], axis=2)

    mod_rows = 2 * SUBLANES
    c_pad = jnp.pad(c, ((0, mod_rows - batch), (0, 0)))
    mod = _modulation(c_pad, w_mod, b_mod)

    xt = x.reshape(batch * seq, d)
    for l in range(depth):
        xt, ffn_w = _ffn(xt, mod, norms, ffn_w, (ffn_f32, l, 1), l, 0, seq)
        zx, qkv, gt = _in_proj(xt, mod, norms, w_in_groups, (F32, BF16, F32), l, seq)
        dt_sp, cum = _gates(gt, gate_bias, l, batch, seq)
        y_ssm = _ssd(zx, dt_sp, conv_w, conv_b3, a_log_pad, d_skip_x, ssm_norm3, l, batch, seq)
        y_attn = _attention(qkv, cum, attn_norm3, l, batch, seq, q_blk, k_blk, v_blk)
        xt = _out_proj(xt, y_ssm, y_attn, wo, mod, norms, l, seq)
        following = (ffn_f32, l + 1, 0) if l + 1 < depth else None
        xt, ffn_w = _ffn(xt, mod, norms, ffn_w, following, l, 2, seq)
    return xt.reshape(batch, seq, d)
```

```python
import functools

import jax
import jax.numpy as jnp
from jax import lax
from jax.experimental import pallas as pl
from jax.experimental.pallas import tpu as pltpu

F32 = jnp.float32
BF16 = jnp.bfloat16

EPS = 1e-6
N_MOD = 9
SSM_HEAD_DIM = 64
SSM_HEADS = 16
SSM_GROUPS = 2
SSM_STATE = 128
CONV_WIDTH = 4
ATTN_HEAD_DIM = 128
ATTN_HEADS = 8
LANES = 128
SUBLANES = 8
VMEM_LIMIT_BYTES = 56 * 1024 * 1024

FFN_TM, FFN_TF, FFN_TN = 512, 512, 512
ROW_CHUNK, ROW_UNROLL = 16, 4
PROJ_TM = 512
OUT_TM = 512
MOD_TN = 1024
SPLIT_COLS = 256
SSD_CHUNK = 128
ATTN_TQ = 512
LOG2E = 1.4426950408889634
GATE_BLK = 128


def _params(*sem):
    return pltpu.CompilerParams(dimension_semantics=sem, vmem_limit_bytes=VMEM_LIMIT_BYTES)


def _silu(x):
    return x / (1.0 + jnp.exp(-x))


def _rms(x, g):
    return x * lax.rsqrt(jnp.mean(x * x, axis=-1, keepdims=True) + EPS) * g


def _norm_mod(x, g, shift, scale):
    return _rms(x, g) * (1.0 + scale) + shift


def _split3(x):
    hi = x.astype(BF16)
    r = x - hi.astype(F32)
    mid = r.astype(BF16)
    lo = (r - mid.astype(F32)).astype(BF16)
    return hi, mid, lo


def _dot(a, b):
    return jnp.dot(a, b, preferred_element_type=F32)


def _dot3(m01, x):
    hi, mid, lo = _split3(x)
    return _dot(m01, hi) + _dot(m01, mid) + _dot(m01, lo)


def _dot3r(x, m01):
    hi, mid, lo = _split3(x)
    return _dot(hi, m01) + _dot(mid, m01) + _dot(lo, m01)


def _tril_ones(n):
    r = lax.broadcasted_iota(jnp.int32, (n, n), 0)
    c = lax.broadcasted_iota(jnp.int32, (n, n), 1)
    return r >= c


def _mod_kernel(c_ref, w_ref, b_ref, o_ref):
    ca = _silu(c_ref[...]).astype(BF16)
    o_ref[...] = _dot(ca, w_ref[...].astype(BF16)) + b_ref[...]


def _modulation(c_pad, w_mod, b_mod):
    depth, d, n = w_mod.shape
    rows = c_pad.shape[0]
    per_piece = d // MOD_TN
    mod = pl.pallas_call(
        _mod_kernel,
        grid=(depth, n // MOD_TN),
        in_specs=[
            pl.BlockSpec((rows, d), lambda l, j: (0, 0)),
            pl.BlockSpec((None, d, MOD_TN), lambda l, j: (l, 0, j)),
            pl.BlockSpec((None, 1, MOD_TN), lambda l, j: (l, 0, j)),
        ],
        out_specs=pl.BlockSpec((None, None, rows, MOD_TN), lambda l, j: (l, j // per_piece, 0, j % per_piece)),
        out_shape=jax.ShapeDtypeStruct((depth, N_MOD, rows, d), F32),
        compiler_params=_params("arbitrary", "arbitrary"),
        name="modulation",
    )(c_pad, w_mod, b_mod.reshape(depth, 1, n))
    return mod.reshape(depth, N_MOD // 3, 3, rows, d)


def _mod_spec(layer, sub, rows, d, grid_rank):
    zeros = (0,) * 3
    index = (lambda i: (layer, sub) + zeros) if grid_rank == 1 else (lambda i, s: (layer, sub) + zeros)
    return pl.BlockSpec((None, None, 3, rows, d), index)


def _norm_spec(layer, sub, d, grid_rank):
    index = (lambda i: (layer, sub, 0, 0)) if grid_rank == 1 else (lambda i, s: (layer, sub, 0, 0))
    return pl.BlockSpec((None, None, 2, d), index)


def _norm_mod_rows(x_ref, norm_ref, mod_ref, b, gs_ref, h_ref):
    gs_ref[0:1, :] = norm_ref[0:1, :] * (1.0 + mod_ref[1, pl.ds(b, 1), :])
    gs_ref[1:2, :] = mod_ref[0, pl.ds(b, 1), :]

    def body(r, carry):
        rows = pl.ds(pl.multiple_of(r * ROW_CHUNK, ROW_CHUNK), ROW_CHUNK)
        xv = x_ref[rows, :]
        rs = lax.rsqrt(jnp.mean(xv * xv, axis=-1, keepdims=True) + EPS)
        h_ref[rows, :] = (xv * rs * gs_ref[0:1, :] + gs_ref[1:2, :]).astype(BF16)
        return carry

    lax.fori_loop(0, x_ref.shape[0] // ROW_CHUNK, body, 0, unroll=ROW_UNROLL)


def _ffn_kernel(x_ref, norm_ref, mod_ref, wg_ref, wu_ref, wd_ref, *rest, cast_next, blocks_per_row):
    if cast_next:
        nwg_ref, nwu_ref, nwd_ref, o_ref, cwg_ref, cwu_ref, cwd_ref, h_ref, a_ref, y_ref, gs_ref = rest
    else:
        o_ref, h_ref, a_ref, y_ref, gs_ref = rest
    s = pl.program_id(1)
    b = pl.program_id(0) // blocks_per_row
    nf, tm, _ = a_ref.shape
    nd, _, tn = y_ref.shape
    d = x_ref.shape[1]

    @pl.when(s == 0)
    def _():
        _norm_mod_rows(x_ref, norm_ref, mod_ref, b, gs_ref, h_ref)

    @pl.when(s < nf)
    def _():
        h = h_ref[...]
        g = _dot(h, wg_ref[...])
        u = _dot(h, wu_ref[...])
        a_ref[s] = (_silu(g) * u).astype(BF16)
        if cast_next:
            cwd_ref[...] = nwd_ref[...].astype(BF16)

    @pl.when(s >= nf)
    def _():
        a = jnp.concatenate([a_ref[k] for k in range(nf)], axis=1)
        y_ref[s - nf] = _dot(a, wd_ref[...])
        if cast_next:
            cwg_ref[...] = nwg_ref[...].astype(BF16)
            cwu_ref[...] = nwu_ref[...].astype(BF16)

    @pl.when(s == nf + nd - 1)
    def _():
        gs_ref[0:1, :] = (0.5 * mod_ref[2, pl.ds(b, 1), :]) * norm_ref[1:2, :]

        def body(r, carry):
            rows = pl.ds(pl.multiple_of(r * ROW_CHUNK, ROW_CHUNK), ROW_CHUNK)
            ys = [y_ref[n, rows, :] for n in range(nd)]
            ss = sum(jnp.sum(y * y, axis=-1, keepdims=True) for y in ys)
            rs = lax.rsqrt(ss * (1.0 / d) + EPS)
            for n in range(nd):
                cols = slice(n * tn, (n + 1) * tn)
                o_ref[rows, cols] = x_ref[rows, cols] + ys[n] * rs * gs_ref[0:1, cols]
            return carry

        lax.fori_loop(0, tm // ROW_CHUNK, body, 0, unroll=ROW_UNROLL)


def _ffn(x, mod, norms, weights, next_weights, layer, sub, seq):
    wg, wu, wd = weights
    t, d = x.shape
    dff = wg.shape[-1]
    nf, nd = dff // FFN_TF, d // FFN_TN
    nblk = t // FFN_TM
    up_tile = lambda i, s: (0, jnp.minimum(s, nf - 1))
    in_specs = [
        pl.BlockSpec((FFN_TM, d), lambda i, s: (i, 0)),
        _norm_spec(layer, sub, d, 2),
        _mod_spec(layer, sub, mod.shape[-2], d, 2),
        pl.BlockSpec((d, FFN_TF), up_tile),
        pl.BlockSpec((d, FFN_TF), up_tile),
        pl.BlockSpec((dff, FFN_TN), lambda i, s: (0, jnp.maximum(s - nf, 0))),
    ]
    out_specs = [pl.BlockSpec((FFN_TM, d), lambda i, s: (i, 0))]
    out_shape = [jax.ShapeDtypeStruct((t, d), F32)]
    operands = [x, norms, mod, wg, wu, wd]
    if next_weights is not None:
        (nwg, nwu, nwd), nl, ns = next_weights
        up_rows, dn_rows = d // (nblk * nd), dff // (nblk * nf)
        assert up_rows % (2 * SUBLANES) == 0 and dn_rows % (2 * SUBLANES) == 0
        up_slab = lambda i, s: i * nd + jnp.clip(s - nf, 0, nd - 1)
        dn_slab = lambda i, s: i * nf + jnp.minimum(s, nf - 1)
        in_specs += [
            pl.BlockSpec((None, None, up_rows, dff), lambda i, s: (nl, ns, up_slab(i, s), 0)),
            pl.BlockSpec((None, None, up_rows, dff), lambda i, s: (nl, ns, up_slab(i, s), 0)),
            pl.BlockSpec((None, None, dn_rows, d), lambda i, s: (nl, ns, dn_slab(i, s), 0)),
        ]
        out_specs += [
            pl.BlockSpec((up_rows, dff), lambda i, s: (up_slab(i, s), 0)),
            pl.BlockSpec((up_rows, dff), lambda i, s: (up_slab(i, s), 0)),
            pl.BlockSpec((dn_rows, d), lambda i, s: (dn_slab(i, s), 0)),
        ]
        out_shape += [jax.ShapeDtypeStruct((d, dff), BF16), jax.ShapeDtypeStruct((d, dff), BF16),
                      jax.ShapeDtypeStruct((dff, d), BF16)]
        operands += [nwg, nwu, nwd]
    outs = pl.pallas_call(
        functools.partial(_ffn_kernel, cast_next=next_weights is not None, blocks_per_row=seq // FFN_TM),
        grid=(nblk, nf + nd),
        in_specs=in_specs,
        out_specs=out_specs,
        out_shape=out_shape,
        scratch_shapes=[
            pltpu.VMEM((FFN_TM, d), BF16),
            pltpu.VMEM((nf, FFN_TM, FFN_TF), BF16),
            pltpu.VMEM((nd, FFN_TM, FFN_TN), F32),
            pltpu.VMEM((2, d), F32),
        ],
        compiler_params=_params("arbitrary", "arbitrary"),
        name="swiglu_half_step",
    )(*operands)
    return outs[0], tuple(outs[1:])


def _split_w_in(w_in_t, groups):
    depth, n, d = w_in_t.shape
    widths = [-(-sum(hi - lo for lo, hi in ranges) // LANES) * LANES for ranges in groups]

    def body(w_ref, *o_refs):
        for ranges, width, o_ref in zip(groups, widths, o_refs):
            pieces = [w_ref[lo:hi, :] for lo, hi in ranges]
            have = sum(hi - lo for lo, hi in ranges)
            if have < width:
                pieces.append(jnp.zeros((width - have, SPLIT_COLS), F32))
            rows = pieces[0] if len(pieces) == 1 else jnp.concatenate(pieces, axis=0)
            o_ref[...] = rows.T.astype(BF16)

    return pl.pallas_call(
        body,
        grid=(depth, d // SPLIT_COLS),
        in_specs=[pl.BlockSpec((None, n, SPLIT_COLS), lambda l, c: (l, 0, c))],
        out_specs=[pl.BlockSpec((None, SPLIT_COLS, w), lambda l, c: (l, c, 0)) for w in widths],
        out_shape=[jax.ShapeDtypeStruct((depth, d, w), BF16) for w in widths],
        compiler_params=_params("arbitrary", "arbitrary"),
        name="split_w_in",
    )(w_in_t)


def _in_proj(x, mod, norms, weights, out_dtypes, layer, seq):
    t, d = x.shape
    per_b = seq // PROJ_TM
    nw = len(weights)

    def body(x_ref, norm_ref, mod_ref, *rest):
        w_refs, o_refs, (h_ref, gs_ref) = rest[:nw], rest[nw:2 * nw], rest[2 * nw:]
        _norm_mod_rows(x_ref, norm_ref, mod_ref, pl.program_id(0) // per_b, gs_ref, h_ref)
        h = h_ref[...]
        for w_ref, o_ref in zip(w_refs, o_refs):
            o_ref[...] = _dot(h, w_ref[...]).astype(o_ref.dtype)

    return pl.pallas_call(
        body,
        grid=(t // PROJ_TM,),
        in_specs=[
            pl.BlockSpec((PROJ_TM, d), lambda i: (i, 0)),
            _norm_spec(layer, 1, d, 1),
            _mod_spec(layer, 1, mod.shape[-2], d, 1),
        ] + [pl.BlockSpec((None,) + w.shape[1:], lambda i: (layer, 0, 0), pipeline_mode=pl.Buffered(1))
             for w in weights],
        out_specs=[pl.BlockSpec((PROJ_TM, w.shape[-1]), lambda i: (i, 0)) for w in weights],
        out_shape=[jax.ShapeDtypeStruct((t, w.shape[-1]), dt) for w, dt in zip(weights, out_dtypes)],
        scratch_shapes=[pltpu.VMEM((PROJ_TM, d), BF16), pltpu.VMEM((2, d), F32)],
        compiler_params=_params("arbitrary"),
        name="mixer_in_proj",
    )(x, norms, mod, *weights)


def _gates_kernel(p_ref, bias_ref, dt_ref, cum_ref):
    seq = p_ref.shape[0]
    tril = _tril_ones(GATE_BLK).astype(BF16)
    carry = jnp.zeros((1, LANES), F32)
    for blk in range(seq // GATE_BLK):
        rows = pl.ds(blk * GATE_BLK, GATE_BLK)
        u = p_ref[rows, :] + bias_ref[...]
        t = jnp.log1p(jnp.exp(-jnp.abs(u)))
        dt_ref[rows, :] = jnp.maximum(u, 0.0) + t
        log_f = jnp.minimum(u, 0.0) - t
        cs = _dot3(tril, log_f) + carry
        cum_ref[rows, :] = cs
        carry = cs[GATE_BLK - 1:GATE_BLK, :]


def _gates(proj, gate_bias, layer, batch, seq):
    t = proj.shape[0]
    return pl.pallas_call(
        _gates_kernel,
        grid=(batch,),
        in_specs=[
            pl.BlockSpec((seq, LANES), lambda b: (b, 0)),
            pl.BlockSpec((None, 1, LANES), lambda b: (layer, 0, 0)),
        ],
        out_specs=[
            pl.BlockSpec((seq, LANES), lambda b: (b, 0)),
            pl.BlockSpec((seq, LANES), lambda b: (b, 0)),
        ],
        out_shape=[
            jax.ShapeDtypeStruct((t, LANES), F32),
            jax.ShapeDtypeStruct((t, LANES), F32),
        ],
        compiler_params=_params("arbitrary"),
        name="mixer_gates",
    )(proj, gate_bias)


def _ssd_kernel(xs_ref, bc_ref, z_ref, dt_ref, cw_ref, cb_ref, alog_ref, dskip_ref, gn_ref,
                o_ref, ubuf_ref, state_ref, ydiag_ref, expand_ref):
    lc = xs_ref.shape[0]
    d_ssm = xs_ref.shape[1]
    gw = d_ssm // SSM_GROUPS
    hpg = SSM_HEADS // SSM_GROUPS

    @pl.when(pl.program_id(1) == 0)
    def _():
        ubuf_ref[0:SUBLANES, :] = jnp.zeros((SUBLANES, ubuf_ref.shape[1]), F32)
        state_ref[...] = jnp.zeros(state_ref.shape, F32)
        er = lax.broadcasted_iota(jnp.int32, (LANES, d_ssm), 0)
        ec = lax.broadcasted_iota(jnp.int32, (LANES, d_ssm), 1)
        expand_ref[...] = (ec // SSM_HEAD_DIM == er).astype(BF16)

    ubuf_ref[SUBLANES:, 0:d_ssm] = xs_ref[...]
    ubuf_ref[SUBLANES:, d_ssm:] = bc_ref[...]
    u_ext = ubuf_ref[...]
    acc = cw_ref[0:1, :] * u_ext
    for k in range(1, CONV_WIDTH):
        acc = pltpu.roll(acc, 1, axis=0) + cw_ref[k:k + 1, :] * u_ext
    ubuf_ref[0:SUBLANES, :] = ubuf_ref[lc:lc + SUBLANES, :]
    xbc = _silu(acc[SUBLANES:, :] + cb_ref[...])
    xs = xbc[:, 0:d_ssm]
    b_mat = xbc[:, d_ssm:d_ssm + SSM_GROUPS * SSM_STATE].astype(BF16)
    c_mat = xbc[:, d_ssm + SSM_GROUPS * SSM_STATE:].astype(BF16)

    lane = lax.broadcasted_iota(jnp.int32, (1, LANES), 1)
    a = jnp.where(lane < SSM_HEADS, -jnp.exp(alog_ref[...]), 0.0)
    dt = dt_ref[...]
    tri = _tril_ones(lc)
    a_cs = _dot3(tri.astype(BF16), dt * a)
    a_cs_t = a_cs.T
    a_last = a_cs[lc - 1:lc, :]

    expand = expand_ref[...]
    dt_x = _dot3r(dt, expand)
    dec_in = _dot3r(jnp.exp(a_cs), expand)
    dec_out = _dot3r(jnp.exp(a_last - a_cs), expand)
    xdt = xs * dt_x

    for g in range(SSM_GROUPS):
        cg = c_mat[:, g * SSM_STATE:(g + 1) * SSM_STATE]
        bg = b_mat[:, g * SSM_STATE:(g + 1) * SSM_STATE]
        cb = lax.dot_general(cg, bg, (((1,), (1,)), ((), ())), preferred_element_type=F32)
        for hh in range(hpg):
            h = g * hpg + hh
            seg = a_cs[:, h:h + 1] - a_cs_t[h:h + 1, :]
            m = (cb * jnp.exp(jnp.where(tri, seg, -jnp.inf))).astype(BF16)
            cols = slice(h * SSM_HEAD_DIM, (h + 1) * SSM_HEAD_DIM)
            ydiag_ref[:, cols] = _dot(m, xdt[:, cols].astype(BF16))

    xw = (xdt * dec_out).astype(BF16)
    y_parts = []
    for g in range(SSM_GROUPS):
        cols = slice(g * gw, (g + 1) * gw)
        cg = c_mat[:, g * SSM_STATE:(g + 1) * SSM_STATE]
        bg = b_mat[:, g * SSM_STATE:(g + 1) * SSM_STATE]
        st = state_ref[g]
        y_parts.append(_dot(cg, st.astype(BF16)))
        upd = lax.dot_general(bg, xw[:, cols], (((0,), (0,)), ((), ())), preferred_element_type=F32)
        state_ref[g] = st * dec_in[lc - 1:lc, cols] + upd
    y_off = jnp.concatenate(y_parts, axis=1) * dec_in

    y = (ydiag_ref[...] + y_off + dskip_ref[...] * xs) * _silu(z_ref[...])
    outs = []
    for g in range(SSM_GROUPS):
        cols = slice(g * gw, (g + 1) * gw)
        outs.append(_rms(y[:, cols], gn_ref[:, cols]))
    o_ref[...] = jnp.concatenate(outs, axis=1).astype(o_ref.dtype)


def _ssd(proj, dt_sp, conv_w, conv_b, a_log_pad, d_skip_x, ssm_norm, layer, batch, seq):
    t = proj.shape[0]
    d_ssm = SSM_HEADS * SSM_HEAD_DIM
    d_bc = 2 * SSM_GROUPS * SSM_STATE
    conv_dim = d_ssm + d_bc
    nc = seq // SSD_CHUNK
    row = lambda b, c: b * nc + c
    par = lambda width: pl.BlockSpec((None, 1, width), lambda b, c: (layer, 0, 0))
    return pl.pallas_call(
        _ssd_kernel,
        grid=(batch, nc),
        in_specs=[
            pl.BlockSpec((SSD_CHUNK, d_ssm), lambda b, c: (row(b, c), 1)),
            pl.BlockSpec((SSD_CHUNK, d_bc), lambda b, c: (row(b, c), 2 * d_ssm // d_bc)),
            pl.BlockSpec((SSD_CHUNK, d_ssm), lambda b, c: (row(b, c), 0)),
            pl.BlockSpec((SSD_CHUNK, LANES), lambda b, c: (row(b, c), 0)),
            pl.BlockSpec((None, CONV_WIDTH, conv_dim), lambda b, c: (layer, 0, 0)),
            par(conv_dim), par(LANES), par(d_ssm), par(d_ssm),
        ],
        out_specs=pl.BlockSpec((SSD_CHUNK, d_ssm), lambda b, c: (row(b, c), 0)),
        out_shape=jax.ShapeDtypeStruct((t, d_ssm), BF16),
        scratch_shapes=[
            pltpu.VMEM((SSD_CHUNK + SUBLANES, conv_dim), F32),
            pltpu.VMEM((SSM_GROUPS, SSM_STATE, d_ssm // SSM_GROUPS), F32),
            pltpu.VMEM((SSD_CHUNK, d_ssm), F32),
            pltpu.VMEM((LANES, d_ssm), BF16),
        ],
        compiler_params=_params("arbitrary", "arbitrary"),
        name="ssd_heads",
    )(proj, proj, proj, dt_sp, conv_w, conv_b, a_log_pad, d_skip_x, ssm_norm)


def _attn_kernel(q_ref, k_ref, v_ref, cum_ref, gn_ref, o_ref, qa_ref, ka_ref, vt_ref):
    h = pl.program_id(1)
    seq = q_ref.shape[0]
    tq = ATTN_TQ
    lane = lax.broadcasted_iota(jnp.int32, (1, LANES), 1)
    c2 = LOG2E * jnp.sum(jnp.where(lane == 2 * SUBLANES + h, cum_ref[...], 0.0), axis=1, keepdims=True)
    hi, mid, lo = (piece.astype(F32) for piece in _split3(c2))
    aug_q = jnp.where(lane == 0, hi, jnp.where(lane == 1, mid, jnp.where(lane == 2, lo,
                      jnp.where(lane < 6, 1.0, 0.0))))
    aug_k = jnp.where(lane < 3, 1.0, jnp.where(lane == 3, -hi, jnp.where(lane == 4, -mid,
                      jnp.where(lane == 5, -lo, 0.0))))
    qa_ref[:, 0:LANES] = (q_ref[...].astype(F32) * (LOG2E * ATTN_HEAD_DIM ** -0.5)).astype(BF16)
    qa_ref[:, LANES:] = aug_q.astype(BF16)
    ka_ref[:, 0:LANES] = k_ref[...].astype(BF16)
    ka_ref[:, LANES:] = aug_k.astype(BF16)
    vt_ref[...] = v_ref[...].astype(F32).T.astype(BF16)

    srow = lax.broadcasted_iota(jnp.int32, (tq, tq), 0)
    tcol = lax.broadcasted_iota(jnp.int32, (tq, tq), 1)
    causal = srow <= tcol
    nt = (((1,), (1,)), ((), ()))
    scores = []
    for i in range(seq // tq):
        lo_k, hi_k = i * tq, (i + 1) * tq
        qa = qa_ref[lo_k:hi_k, :]
        s_diag = jnp.where(causal, lax.dot_general(ka_ref[lo_k:hi_k, :], qa, nt, preferred_element_type=F32),
                           -jnp.inf)
        s_past = lax.dot_general(ka_ref[0:lo_k, :], qa, nt, preferred_element_type=F32) if i > 0 else None
        scores.append((s_diag, s_past))
    for i, (s_diag, s_past) in enumerate(scores):
        lo_k, hi_k = i * tq, (i + 1) * tq
        m = jnp.max(s_diag, axis=0, keepdims=True)
        if i > 0:
            m = jnp.maximum(m, jnp.max(s_past, axis=0, keepdims=True))
        p = jnp.exp2(s_diag - m)
        if i > 0:
            p = jnp.concatenate([jnp.exp2(s_past - m), p], axis=0)
        l = jnp.sum(p, axis=0, keepdims=True)
        o_t = _dot(vt_ref[:, 0:hi_k], p.astype(BF16)) / l
        o_t = o_t * lax.rsqrt(jnp.mean(o_t * o_t, axis=0, keepdims=True) + EPS)
        o_ref[lo_k:hi_k, :] = (o_t.T * gn_ref[...]).astype(o_ref.dtype)


def _attention(proj, cum, attn_norm, layer, batch, seq, q_blk, k_blk, v_blk):
    t = proj.shape[0]
    d_attn = ATTN_HEADS * ATTN_HEAD_DIM
    col = lambda blk: pl.BlockSpec((seq, ATTN_HEAD_DIM), lambda b, h: (b, blk + h))
    return pl.pallas_call(
        _attn_kernel,
        grid=(batch, ATTN_HEADS),
        in_specs=[
            col(q_blk), col(k_blk), col(v_blk),
            pl.BlockSpec((seq, LANES), lambda b, h: (b, 0)),
            pl.BlockSpec((None, 1, ATTN_HEAD_DIM), lambda b, h: (layer, 0, h)),
        ],
        out_specs=pl.BlockSpec((seq, ATTN_HEAD_DIM), lambda b, h: (b, h)),
        out_shape=jax.ShapeDtypeStruct((t, d_attn), BF16),
        scratch_shapes=[
            pltpu.VMEM((seq, 2 * LANES), BF16),
            pltpu.VMEM((seq, 2 * LANES), BF16),
            pltpu.VMEM((ATTN_HEAD_DIM, seq), BF16),
        ],
        compiler_params=_params("arbitrary", "arbitrary"),
        name="forgetting_attention",
    )(proj, proj, proj, cum, attn_norm)


def _out_kernel(x_ref, ys_ref, ya_ref, w_ref, norm_ref, mod_ref, o_ref, wb_ref, *, blocks_per_row):
    @pl.when(pl.program_id(0) == 0)
    def _():
        wb_ref[...] = w_ref[...].astype(BF16)

    gate = mod_ref[2, pl.ds(pl.program_id(0) // blocks_per_row, 1), :]
    h = _dot(jnp.concatenate([ys_ref[...], ya_ref[...]], axis=1), wb_ref[...])
    o_ref[...] = x_ref[...] + gate * _rms(h, norm_ref[1:2, :])


def _out_proj(x, y_ssm, y_attn, w_out, mod, norms, layer, seq):
    t, d = x.shape
    return pl.pallas_call(
        functools.partial(_out_kernel, blocks_per_row=seq // OUT_TM),
        grid=(t // OUT_TM,),
        in_specs=[
            pl.BlockSpec((OUT_TM, d), lambda i: (i, 0)),
            pl.BlockSpec((OUT_TM, y_ssm.shape[1]), lambda i: (i, 0)),
            pl.BlockSpec((OUT_TM, y_attn.shape[1]), lambda i: (i, 0)),
            pl.BlockSpec((None,) + w_out.shape[1:], lambda i: (layer, 0, 0), pipeline_mode=pl.Buffered(1)),
            _norm_spec(layer, 1, d, 1),
            _mod_spec(layer, 1, mod.shape[-2], d, 1),
        ],
        out_specs=pl.BlockSpec((OUT_TM, d), lambda i: (i, 0)),
        out_shape=jax.ShapeDtypeStruct((t, d), F32),
        scratch_shapes=[pltpu.VMEM(w_out.shape[1:], BF16)],
        compiler_params=_params("arbitrary"),
        name="mixer_out_proj",
    )(x, y_ssm, y_attn, w_out, norms, mod)


def kernel(x, c, norm_pre, norm_post, w_mod, b_mod, w_ffn_gate, w_ffn_up, w_ffn_down, w_in, conv_w,
           conv_b, dt_bias, a_log, d_skip, ssm_norm, f_bias, attn_norm, w_out):
    batch, seq, d = x.shape
    depth = w_mod.shape[0]
    d_ssm = SSM_HEADS * SSM_HEAD_DIM
    d_attn = ATTN_HEADS * ATTN_HEAD_DIM
    conv_dim = d_ssm + 2 * SSM_GROUPS * SSM_STATE
    assert seq % max(FFN_TM, PROJ_TM, OUT_TM, ATTN_TQ, SSD_CHUNK) == 0
    assert w_in.shape[-1] == 2 * d_ssm + 2 * SSM_GROUPS * SSM_STATE + SSM_HEADS + 3 * d_attn + ATTN_HEADS

    ffn_f32 = (w_ffn_gate, w_ffn_up, w_ffn_down)
    ffn_w = tuple(w[0, 0].astype(BF16) for w in ffn_f32)
    o_dt = d_ssm + conv_dim
    o_q = o_dt + SSM_HEADS
    o_f = o_q + 3 * d_attn
    pad = LANES - SSM_HEADS - ATTN_HEADS
    w_in_groups = _split_w_in(jnp.swapaxes(w_in, 1, 2),
                              (((0, o_dt),), ((o_q, o_f),), ((o_dt, o_q), (o_f, w_in.shape[-1]))))
    q_blk, k_blk, v_blk = 0, d_attn // LANES, 2 * d_attn // LANES

    gate_bias = jnp.concatenate([dt_bias, f_bias, jnp.zeros((depth, pad), F32)], axis=-1)[:, None, :]
    a_log_pad = jnp.pad(a_log, ((0, 0), (0, LANES - SSM_HEADS)))[:, None, :]
    d_skip_x = jnp.repeat(d_skip, SSM_HEAD_DIM, axis=-1)[:, None, :]
    conv_b3 = conv_b[:, None, :]
    ssm_norm3 = ssm_norm[:, None, :]
    attn_norm3 = attn_norm[:, None, :]
    norms = jnp.stack([norm_pre, norm_post], axis=2)

    mod_rows = 2 * SUBLANES
    c_pad = jnp.pad(c, ((0, mod_rows - batch), (0, 0)))
    mod = _modulation(c_pad, w_mod, b_mod)

    xt = x.reshape(batch * seq, d)
    for l in range(depth):
        xt, ffn_w = _ffn(xt, mod, norms, ffn_w, (ffn_f32, l, 1), l, 0, seq)
        zx, qkv, gt = _in_proj(xt, mod, norms, w_in_groups, (F32, BF16, F32), l, seq)
        dt_sp, cum = _gates(gt, gate_bias, l, batch, seq)
        y_ssm = _ssd(zx, dt_sp, conv_w, conv_b3, a_log_pad, d_skip_x, ssm_norm3, l, batch, seq)
        y_attn = _attention(qkv, cum, attn_norm3, l, batch, seq, q_blk, k_blk, v_blk)
        xt = _out_proj(xt, y_ssm, y_attn, w_out, mod, norms, l, seq)
        following = (ffn_f32, l + 1, 0) if l + 1 < depth else None
        xt, ffn_w = _ffn(xt, mod, norms, ffn_w, following, l, 2, seq)
    return xt.reshape(batch, seq, d)
```
